```python
import functools
import jax, jax.numpy as jnp
from jax import lax
import numpy as np

D_MODEL = 1024
BATCH = 2
SEQ = 8192
DEPTH = 1
DEC_BATCH = 128
DEC_SEQ = 4
PAST_LEN = 2048
PAGE_SIZE = 128

HG_HEADS = 4
HG_DK = 128
HG_DV = 128
HG_WIDTH = HG_HEADS * HG_DV
HG_CHUNK = 64
AT_HEADS = 4
AT_DH = 128
AT_WIDTH = AT_HEADS * AT_DH
IDX_HEADS = 8
IDX_DIM = 64
IDX_SCALE = (IDX_HEADS * IDX_DIM) ** -0.5
TOPK_MAX = 256
Q_BLOCK = 128
D_FF = 4 * D_MODEL
EPS = 1e-6

_WIDTHS = (HG_HEADS * HG_DK, HG_HEADS * HG_DK, HG_WIDTH, HG_WIDTH,
           AT_WIDTH, AT_WIDTH, AT_WIDTH,
           IDX_HEADS * IDX_DIM, IDX_DIM, IDX_HEADS,
           D_MODEL, D_MODEL)
D_IN = sum(_WIDTHS)

kernel_name = "hgrn2_dsa_gated_hybrid_step"


def _rmsnorm(x, g):
    xf = x.astype(jnp.float32)
    y = xf * lax.rsqrt(jnp.mean(xf * xf, axis=-1, keepdims=True) + EPS) * g.astype(jnp.float32)
    return y.astype(x.dtype)


def _hgrn2(q, k, v, logf, s0):
    f32 = jnp.float32
    B, T, H, DK = q.shape
    DV = v.shape[-1]
    c = HG_CHUNK if T % HG_CHUNK == 0 else T
    n = T // c

    def blocks(t):
        return t.astype(f32).reshape(B, n, c, H, t.shape[-1]).transpose(1, 0, 3, 2, 4)

    tri = jnp.tril(jnp.ones((c, c), dtype=bool))

    def step(S, inp):
        qc, kc, vc, lc = inp
        b = jnp.cumsum(lc, axis=2)
        o = jnp.einsum('bhtd,bhde->bhte', qc * jnp.exp(b), S)
        diff = jnp.where(tri[:, :, None], b[:, :, :, None, :] - b[:, :, None, :, :], -jnp.inf)
        a = jnp.sum(qc[:, :, :, None, :] * kc[:, :, None, :, :] * jnp.exp(diff), axis=-1)
        o = o + jnp.einsum('bhts,bhse->bhte', a, vc)
        b_end = b[:, :, -1:, :]
        S = jnp.exp(b_end[:, :, 0, :, None]) * S + jnp.einsum('bhsd,bhse->bhde', kc * jnp.exp(b_end - b), vc)
        return S, o

    S, o = lax.scan(step, s0.astype(f32), (blocks(q), blocks(k), blocks(v), blocks(logf)))
    o = o.transpose(1, 0, 3, 2, 4).reshape(B, T, H, DV)
    return o.astype(q.dtype), S.astype(s0.dtype)


def _sparse_attend(q, qi, w, qpos, k, v, kidx, n_sel):
    f32 = jnp.float32
    L = k.shape[1]
    dots = jnp.einsum('bqhd,bld->bqhl', qi.astype(f32), kidx.astype(f32))
    score = jnp.einsum('bqhl,bqh->bql', jax.nn.relu(dots), w.astype(f32)) * IDX_SCALE
    allowed = jnp.arange(L)[None, :] <= qpos[:, None]
    score = jnp.where(allowed[None], score, -jnp.inf)
    _, sel = lax.top_k(score, n_sel)
    valid = sel <= qpos[None, :, None]
    take = jax.vmap(lambda t, i: t[i])
    kg = take(k, sel).astype(f32)
    vg = take(v, sel).astype(f32)
    logits = jnp.einsum('bqhd,bqkhd->bqhk', q.astype(f32), kg) * (AT_DH ** -0.5)
    logits = jnp.where(valid[:, :, None, :], logits, -jnp.inf)
    p = jax.nn.softmax(logits, axis=-1)
    return jnp.einsum('bqhk,bqkhd->bqhd', p, vg).astype(q.dtype)


def _attend_prompt(q, k, v, qi, kidx, w):
    B, T = q.shape[0], q.shape[1]
    n_sel = min(TOPK_MAX, T // 4)
    nb = T // Q_BLOCK

    def to_blocks(t):
        return t.reshape(B, nb, Q_BLOCK, *t.shape[2:]).swapaxes(0, 1)

    def blk(args):
        qb, qib, wb, qpos = args
        return _sparse_attend(qb, qib, wb, qpos, k, v, kidx, n_sel)

    qpos = jnp.arange(T).reshape(nb, Q_BLOCK)
    out = lax.map(blk, (to_blocks(q), to_blocks(qi), to_blocks(w), qpos))
    return out.swapaxes(0, 1).reshape(B, T, AT_HEADS, AT_DH)


def _attend_sample(q, k, v, qi, kidx, w, past_k, past_v, past_kidx):
    T = q.shape[1]
    P = past_k.shape[1]
    kf = jnp.concatenate([past_k, k.astype(past_k.dtype)], axis=1)
    vf = jnp.concatenate([past_v, v.astype(past_v.dtype)], axis=1)
    kif = jnp.concatenate([past_kidx, kidx.astype(past_kidx.dtype)], axis=1)
    n_sel = min(TOPK_MAX, (P + T) // 4)
    qpos = P + jnp.arange(T)
    return _sparse_attend(q, qi, w, qpos, kf, vf, kif, n_sel)


def _layer(x, s0, attend, lb, w_in, hg_norm, w_a, w_b, w_o, norm1, norm2, w_up, w_down):
    B, T, _ = x.shape
    h = _rmsnorm(x, norm1)
    z = h @ w_in
    offs = np.cumsum(_WIDTHS)[:-1].tolist()
    hq, hf, hi, hog, aq, ak, av, iq, ik, iw, ga, gb = jnp.split(z, offs, axis=-1)

    lbf = lb.astype(jnp.float32)
    logf = jnp.logaddexp(jnp.log(lbf), jnp.log1p(-lbf) + jax.nn.log_sigmoid(hf.astype(jnp.float32)))
    kk = -jnp.expm1(logf)
    hd = lambda t, d: t.reshape(B, T, HG_HEADS, d)
    o, s_new = _hgrn2(hd(hq, HG_DK), hd(kk, HG_DK), hd(hi, HG_DV), hd(logf, HG_DK), s0)
    o = _rmsnorm(o, hg_norm) * jax.nn.silu(hd(hog, HG_DV))
    y_a = o.reshape(B, T, HG_WIDTH) @ w_a

    ak = ak.reshape(B, T, AT_HEADS, AT_DH)
    av = av.reshape(B, T, AT_HEADS, AT_DH)
    ao = attend(aq.reshape(B, T, AT_HEADS, AT_DH), ak, av,
                iq.reshape(B, T, IDX_HEADS, IDX_DIM), ik, iw)
    y_b = ao.reshape(B, T, AT_WIDTH) @ w_b

    m = jax.nn.sigmoid(ga) * y_a + jax.nn.sigmoid(gb) * y_b
    x = x + m @ w_o
    h2 = _rmsnorm(x, norm2)
    x = x + jnp.square(jax.nn.relu(h2 @ w_up)) @ w_down
    return x, s_new, ak, av, ik


def setup_inputs(seed: int = 0) -> dict:
    key = jax.random.key(seed)
    ks = jax.random.split(key, 24)
    n_pages = PAST_LEN // PAGE_SIZE
    n_used = DEC_BATCH * n_pages
    n_phys = n_used + n_used // 4
    nrm = lambda k, shape, s: jax.random.normal(k, shape, jnp.float32) * s
    page_table = jax.random.permutation(ks[0], n_phys)[:n_used].reshape(DEC_BATCH, n_pages).astype(jnp.int32)
    return {
        "x_prompt": nrm(ks[1], (BATCH, SEQ, D_MODEL), 1.0),
        "x_sample": nrm(ks[2], (DEC_BATCH, DEC_SEQ, D_MODEL), 1.0),
        "cache_k": nrm(ks[3], (DEPTH, n_phys, PAGE_SIZE, AT_HEADS, AT_DH), 1.0),
        "cache_v": nrm(ks[4], (DEPTH, n_phys, PAGE_SIZE, AT_HEADS, AT_DH), 1.0),
        "cache_idx_k": nrm(ks[5], (DEPTH, n_phys, PAGE_SIZE, IDX_DIM), 1.0),
        "state_hgrn": nrm(ks[6], (DEPTH, DEC_BATCH, HG_HEADS, HG_DK, HG_DV), 0.3),
        "page_table": page_table,
        "lb_logits": nrm(ks[7], (DEPTH + 1, HG_HEADS * HG_DK), 0.5),
        "w_in": nrm(ks[8], (DEPTH, D_MODEL, D_IN), D_MODEL ** -0.5),
        "hg_norm": 1.0 + nrm(ks[9], (DEPTH, HG_DV), 0.01),
        "w_a": nrm(ks[10], (DEPTH, HG_WIDTH, D_MODEL), HG_WIDTH ** -0.5),
        "w_b": nrm(ks[11], (DEPTH, AT_WIDTH, D_MODEL), AT_WIDTH ** -0.5),
        "w_o": nrm(ks[12], (DEPTH, D_MODEL, D_MODEL), D_MODEL ** -0.5),
        "norm1": 1.0 + nrm(ks[13], (DEPTH, D_MODEL), 0.01),
        "norm2": 1.0 + nrm(ks[14], (DEPTH, D_MODEL), 0.01),
        "w_up": nrm(ks[15], (DEPTH, D_MODEL, D_FF), D_MODEL ** -0.5),
        "w_down": nrm(ks[16], (DEPTH, D_FF, D_MODEL), D_FF ** -0.5),
        "norm_f": 1.0 + nrm(ks[17], (D_MODEL,), 0.01),
    }


def reference(x_prompt, x_sample, cache_k, cache_v, cache_idx_k, state_hgrn, page_table,
              lb_logits, w_in, hg_norm, w_a, w_b, w_o, norm1, norm2, w_up, w_down, norm_f):
    lb_all = jnp.cumsum(jax.nn.softmax(lb_logits.astype(jnp.float32), axis=0), axis=0)
    n_seq = page_table.shape[0]
    xp, xs = x_prompt, x_sample
    kp_l, vp_l, ip_l, sp_l, ks_l, vs_l, is_l, ss_l = [], [], [], [], [], [], [], []
    for l in range(DEPTH):
        prm = (lb_all[l], w_in[l], hg_norm[l], w_a[l], w_b[l], w_o[l], norm1[l], norm2[l], w_up[l], w_down[l])
        s0 = jnp.zeros((xp.shape[0], HG_HEADS, HG_DK, HG_DV), state_hgrn.dtype)
        xp, sp, kp, vp, ip = _layer(xp, s0, _attend_prompt, *prm)
        gather = lambda c: c[l][page_table].reshape(n_seq, -1, *c.shape[3:])
        attend_s = functools.partial(_attend_sample, past_k=gather(cache_k), past_v=gather(cache_v),
                                     past_kidx=gather(cache_idx_k))
        xs, ss, ks_, vs, is_ = _layer(xs, state_hgrn[l], attend_s, *prm)
        kp_l.append(kp); vp_l.append(vp); ip_l.append(ip); sp_l.append(sp)
        ks_l.append(ks_); vs_l.append(vs); is_l.append(is_); ss_l.append(ss)
    y_prompt = _rmsnorm(xp, norm_f)
    y_sample = _rmsnorm(xs, norm_f)
    return (y_prompt, y_sample,
            jnp.stack(kp_l), jnp.stack(vp_l), jnp.stack(ip_l), jnp.stack(sp_l),
            jnp.stack(ks_l), jnp.stack(vs_l), jnp.stack(is_l), jnp.stack(ss_l))
```

```python
import functools

import jax
import jax.numpy as jnp
from jax import lax
from jax.experimental import pallas as pl
from jax.experimental.pallas import tpu as pltpu

F32 = jnp.float32
BF16 = jnp.bfloat16
I32 = jnp.int32

EPS = 1e-6
HG_HEADS = 4
HG_D = 128
HG_W = HG_HEADS * HG_D
HG_CHUNK = 64
AT_HEADS = 4
AT_DH = 128
AT_W = AT_HEADS * AT_DH
IDX_HEADS = 8
IDX_DIM = 64
IDX_W = IDX_HEADS * IDX_DIM
IDX_SCALE = (IDX_HEADS * IDX_DIM) ** -0.5
TOPK_MAX = 256
LANES = 128
INT_MIN = -(2 ** 31)
NEG_BIG = -1e30
VMEM_LIMIT = 56 * 1024 * 1024
ROW_TILE = 512

_C_ZH = 0
_C_AQ = 2048
_C_AK = 2560
_C_AV = 3072
_C_IQ = 3584
_C_IK = 4096
_C_IW = 4224
_C_GG = 4352
_C_END = 6400


def _dot(a, b, **kw):
    return jnp.dot(a, b, preferred_element_type=F32, **kw)


def _dot_nt(a, b):
    return lax.dot_general(a, b, (((1,), (1,)), ((), ())), preferred_element_type=F32)


def _dot_tn(a, b):
    return lax.dot_general(a, b, (((0,), (0,)), ((), ())), preferred_element_type=F32)


def _rms(x, g):
    return x * lax.rsqrt(jnp.mean(x * x, axis=-1, keepdims=True) + EPS) * g


def _sigmoid(x):
    return 1.0 / (1.0 + jnp.exp(-x))


def _cparams(sem):
    return pltpu.CompilerParams(dimension_semantics=sem, vmem_limit_bytes=VMEM_LIMIT)


def _inproj_kernel(x_ref, g_ref, w_ref, zh_ref, aq_ref, ak_ref, av_ref, akb_ref, avb_ref,
                   iq_ref, ik_ref, ikb_ref, iw_ref, gg_ref):
    h = _rms(x_ref[...], g_ref[...]).astype(BF16)

    def mm(a, n):
        return _dot(h, w_ref[:, a:a + n])

    for c in range(4):
        zh_ref[:, c * 512:(c + 1) * 512] = mm(_C_ZH + c * 512, 512)
    aq_ref[...] = mm(_C_AQ, AT_W).astype(BF16)
    k = mm(_C_AK, AT_W)
    ak_ref[...] = k
    akb_ref[...] = k.astype(BF16)
    v = mm(_C_AV, AT_W)
    av_ref[...] = v
    avb_ref[...] = v.astype(BF16)
    iq_ref[...] = mm(_C_IQ, IDX_W).astype(BF16)
    ik = mm(_C_IK, LANES)[:, :IDX_DIM]
    ik_ref[...] = ik
    ikb_ref[...] = ik.astype(BF16)
    iw_ref[...] = mm(_C_IW, LANES)[:, :IDX_HEADS]
    for c in range(4):
        gg_ref[:, c * 512:(c + 1) * 512] = mm(_C_GG + c * 512, 512)


def _inproj(x, g, w, tm):
    rows, d = x.shape
    row = lambda n: pl.BlockSpec((tm, n), lambda i: (i, 0))
    outs = [(2048, F32), (AT_W, BF16), (AT_W, F32), (AT_W, F32), (AT_W, BF16), (AT_W, BF16),
            (IDX_W, BF16), (IDX_DIM, F32), (IDX_DIM, BF16), (IDX_HEADS, F32), (2048, F32)]
    return pl.pallas_call(
        _inproj_kernel,
        grid=(rows // tm,),
        in_specs=[row(d), pl.BlockSpec((1, d), lambda i: (0, 0)),
                  pl.BlockSpec(w.shape, lambda i: (0, 0), pipeline_mode=pl.Buffered(1))],
        out_specs=[row(n) for n, _ in outs],
        out_shape=[jax.ShapeDtypeStruct((rows, n), t) for n, t in outs],
        compiler_params=_cparams(("arbitrary",)),
        name="inproj",
    )(x, g, w)


def _lower_bound(lbl):
    e = jnp.exp(lbl - jnp.max(lbl, axis=0, keepdims=True))
    return e[0:1] / jnp.sum(e, axis=0, keepdims=True)


def _forget(hf, lb):
    f = lb + (1.0 - lb) * _sigmoid(hf)
    return jnp.log(f), 1.0 - f


def _group_row(x, gsize, r):
    n, w = x.shape
    g = x.reshape(n // gsize, gsize, w)
    return jnp.broadcast_to(g[:, r:r + 1, :], g.shape).reshape(n, w)


def _hg_out(o, hog, hgn):
    return (_rms(o, hgn) * (hog * _sigmoid(hog))).astype(BF16)


def _hgrn_prompt_kernel(lbl_ref, hgn_ref, z_ref, o_ref, st_ref, s_scr):
    i = pl.program_id(0)
    nb = z_ref.shape[0]
    c = HG_CHUNK

    @pl.when(i == 0)
    def _():
        s_scr[...] = jnp.zeros_like(s_scr)

    lb = _lower_bound(lbl_ref[...])
    hgn = hgn_ref[...]
    ti = lax.broadcasted_iota(I32, (c, c), 0)
    si = lax.broadcasted_iota(I32, (c, c), 1)
    tril = (si <= ti).astype(F32)
    rowl = lax.broadcasted_iota(I32, (c, 1), 0)
    same = {g: (ti // g) == (si // g) for g in (32, 16, 8)}
    diag_mask = same[8] & (si <= ti)

    for bi in range(nb):
        z = z_ref[bi]
        hq, hf, hi, hog = (z[:, k * HG_W:(k + 1) * HG_W] for k in range(4))
        logf, kk = _forget(hf, lb)
        b = _dot(tril, logf, precision=lax.Precision.HIGHEST)

        qs, ks = [], []
        for m in (32, 16, 8):
            upper = (rowl % (2 * m)) >= m
            e = jnp.exp(-jnp.abs(b - _group_row(b, 2 * m, m - 1)))
            qs.append(jnp.where(upper, hq * e, 0.0).astype(BF16))
            ks.append(jnp.where(upper, 0.0, kk * e).astype(BF16))
        mid = 0.5 * (_group_row(b - logf, 8, 0) + _group_row(b, 8, 7))
        qs.append((hq * jnp.exp(b - mid)).astype(BF16))
        ks.append((kk * jnp.exp(mid - b)).astype(BF16))

        qe = (hq * jnp.exp(b)).astype(BF16)
        bend = b[c - 1:c, :]
        kdec = (kk * jnp.exp(bend - b)).astype(BF16)
        ebend = jnp.exp(bend)
        vb = hi.astype(BF16)

        for h in range(HG_HEADS):
            sl = slice(h * HG_D, (h + 1) * HG_D)
            p32, p16, p8, pd = (_dot_nt(q[:, sl], k[:, sl]) for q, k in zip(qs, ks))
            a = (p32 + jnp.where(same[32], p16, 0.0) + jnp.where(same[16], p8, 0.0)
                 + jnp.where(diag_mask, pd, 0.0))
            st = s_scr[bi, h]
            o = _dot_nt(qe[:, sl], st.astype(BF16)) + _dot(a.astype(BF16), vb[:, sl])
            st_new = st * ebend[:, sl] + _dot_tn(vb[:, sl], kdec[:, sl])
            s_scr[bi, h] = st_new
            o_ref[bi, :, sl] = _hg_out(o, hog[:, sl], hgn)

            @pl.when(i == pl.num_programs(0) - 1)
            def _():
                st_ref[bi, h] = st_new.T


def _hgrn_prompt(lbl, hgn, zh):
    nb, t, _ = zh.shape
    c = HG_CHUNK
    return pl.pallas_call(
        _hgrn_prompt_kernel,
        grid=(t // c,),
        in_specs=[pl.BlockSpec(lbl.shape, lambda i: (0, 0)),
                  pl.BlockSpec(hgn.shape, lambda i: (0, 0)),
                  pl.BlockSpec((nb, c, 4 * HG_W), lambda i: (0, i, 0))],
        out_specs=[pl.BlockSpec((nb, c, HG_W), lambda i: (0, i, 0)),
                   pl.BlockSpec((nb, HG_HEADS, HG_D, HG_D), lambda i: (0, 0, 0, 0))],
        out_shape=[jax.ShapeDtypeStruct((nb, t, HG_W), BF16),
                   jax.ShapeDtypeStruct((nb, HG_HEADS, HG_D, HG_D), F32)],
        scratch_shapes=[pltpu.VMEM((nb, HG_HEADS, HG_D, HG_D), F32)],
        compiler_params=_cparams(("arbitrary",)),
        name="hgrn_prompt",
    )(lbl, hgn, zh)


def _hgrn_sample_kernel(lbl_ref, hgn_ref, z_ref, s0_ref, o_ref, st_ref, *, t):
    rows = z_ref.shape[0]
    nseq = rows // t
    lb = _lower_bound(lbl_ref[...])
    hgn = hgn_ref[...]
    z = z_ref[...]
    hq, hf, hi, hog = (z[:, k * HG_W:(k + 1) * HG_W] for k in range(4))
    logf, kk = _forget(hf, lb)
    tl = lax.broadcasted_iota(I32, (rows, 1), 0) % t

    li = lax.broadcasted_iota(I32, (HG_W, HG_W), 0) // HG_D
    lj = lax.broadcasted_iota(I32, (HG_W, HG_W), 1) // HG_D
    head_ones = (li == lj).astype(BF16)

    shift = lambda x, n: pltpu.roll(x, n % rows, axis=0)
    dsum = jnp.zeros_like(logf)
    b = jnp.zeros_like(logf)
    suf = jnp.zeros_like(logf)
    o_intra = jnp.zeros_like(logf)
    for delta in range(t):
        ok = tl >= delta
        if delta > 0:
            dsum = dsum + shift(logf, delta - 1)
            suf = suf + jnp.where(tl + delta < t, shift(logf, -delta), 0.0)
        b = b + jnp.where(ok, shift(logf, delta), 0.0)
        term = jnp.where(ok, hq * shift(kk, delta) * jnp.exp(dsum), 0.0)
        a = _dot(term.astype(BF16), head_ones)
        o_intra = o_intra + a * shift(hi, delta)

    qe = (hq * jnp.exp(b)).astype(BF16)
    kdec = kk * jnp.exp(suf)
    vb = hi.astype(BF16)
    eb = jnp.exp(b + suf)

    seq_of_row = lax.broadcasted_iota(I32, (rows, 1), 0) // t
    for h in range(HG_HEADS):
        sl = slice(h * HG_D, (h + 1) * HG_D)
        eb_t = eb[:, sl].T
        o = o_intra[:, sl]
        for s in range(nseq):
            mine = seq_of_row == s
            s0 = s0_ref[s, h]
            o = o + jnp.where(mine, _dot(qe[:, sl], s0.astype(BF16)), 0.0)
            kd = jnp.where(mine, kdec[:, sl], 0.0).astype(BF16)
            st_ref[s, h] = s0 * eb_t[:, s * t:s * t + 1] + _dot_tn(kd, vb[:, sl])
        o_ref[:, sl] = _hg_out(o, hog[:, sl], hgn)


def _hgrn_sample(lbl, hgn, zh, s0, t, seq_blk):
    rows = zh.shape[0]
    nseq = rows // t
    rb = seq_blk * t
    return pl.pallas_call(
        functools.partial(_hgrn_sample_kernel, t=t),
        grid=(nseq // seq_blk,),
        in_specs=[pl.BlockSpec(lbl.shape, lambda i: (0, 0)),
                  pl.BlockSpec(hgn.shape, lambda i: (0, 0)),
                  pl.BlockSpec((rb, 4 * HG_W), lambda i: (i, 0)),
                  pl.BlockSpec((seq_blk, HG_HEADS, HG_D, HG_D), lambda i: (i, 0, 0, 0))],
        out_specs=[pl.BlockSpec((rb, HG_W), lambda i: (i, 0)),
                   pl.BlockSpec((seq_blk, HG_HEADS, HG_D, HG_D), lambda i: (i, 0, 0, 0))],
        out_shape=[jax.ShapeDtypeStruct((rows, HG_W), BF16),
                   jax.ShapeDtypeStruct(s0.shape, F32)],
        compiler_params=_cparams(("arbitrary",)),
        name="hgrn_sample",
    )(lbl, hgn, zh, s0)


def _order_key(score):
    bits = lax.bitcast_convert_type(score, I32)
    return bits ^ ((bits >> 31) & 0x7FFFFFFF)


def _bit_search(accept, nbits, start):
    def body(k, x):
        cand = x | lax.shift_left(jnp.int32(1), (nbits - 1 - k).astype(I32))
        return jnp.where(accept(cand), cand, x)
    return lax.fori_loop(0, nbits, body, start)


def _attn_prompt_kernel(aq_ref, iq_ref, iw_ref, kb_ref, vb_ref, ikb_ref, o_ref,
                        key_scr, w_scr, *, n_sel):
    qb = aq_ref.shape[0]
    i = pl.program_id(1)
    nt = i + 1
    qpos = i * qb + lax.broadcasted_iota(I32, (qb, 1), 0)
    lane = lax.broadcasted_iota(I32, (1, LANES), 1)

    iw = iw_ref[...]
    for h in range(IDX_HEADS):
        w_scr[h] = jnp.broadcast_to(iw[:, h:h + 1], (qb, LANES))

    def score_tile(j, carry):
        ikt = ikb_ref[pl.ds(pl.multiple_of(j * LANES, LANES), LANES), :]
        acc = jnp.zeros((qb, LANES), F32)
        for h in range(IDX_HEADS):
            d = _dot_nt(iq_ref[:, h * IDX_DIM:(h + 1) * IDX_DIM], ikt)
            acc = acc + jnp.maximum(d, 0.0) * w_scr[h]
        key = _order_key(acc * IDX_SCALE)
        key_scr[j] = jnp.where(j * LANES + lane <= qpos, key, INT_MIN)
        return carry

    lax.fori_loop(0, nt, score_tile, 0)

    need = jnp.minimum(qpos + 1, n_sel).astype(F32)

    def count(pred):
        def body(j, acc):
            return acc + jnp.where(pred(j, key_scr[j]), 1.0, 0.0)
        acc = lax.fori_loop(0, nt, body, jnp.zeros((qb, LANES), F32))
        return jnp.sum(acc, axis=1, keepdims=True)

    count_ge = lambda x: count(lambda j, key: key >= x)
    zero = jnp.zeros((qb, 1), I32)
    start = jnp.where(count_ge(zero) >= need, 0, INT_MIN).astype(I32)
    thr = _bit_search(lambda x: count_ge(x) >= need, 31, start)

    rest = need - count(lambda j, key: key > thr)
    ties_before = lambda x: count(lambda j, key: (key == thr) & (j * LANES + lane < x))
    nbits = max(1, (kb_ref.shape[0] - 1).bit_length())
    cut = _bit_search(lambda x: ties_before(x) < rest, nbits, zero)

    scale = AT_DH ** -0.5
    for h in range(AT_HEADS):
        sl = slice(h * AT_DH, (h + 1) * AT_DH)
        qh = aq_ref[:, sl]

        def attend(j, carry):
            m, l, acc = carry
            key = key_scr[j]
            sel = (key > thr) | ((key == thr) & (j * LANES + lane <= cut))
            rows = pl.ds(pl.multiple_of(j * LANES, LANES), LANES)
            s = jnp.where(sel, _dot_nt(qh, kb_ref[rows, sl]) * scale, NEG_BIG)
            m_new = jnp.maximum(m, jnp.max(s, axis=1, keepdims=True))
            alpha = jnp.exp(m - m_new)
            p = jnp.exp(s - m_new)
            l = alpha * l + jnp.sum(p, axis=1, keepdims=True)
            acc = alpha * acc + _dot(p.astype(BF16), vb_ref[rows, sl])
            return m_new, l, acc

        m0 = jnp.full((qb, 1), NEG_BIG, F32)
        _, l, acc = lax.fori_loop(0, nt, attend,
                                  (m0, jnp.zeros((qb, 1), F32), jnp.zeros((qb, AT_DH), F32)))
        o_ref[:, sl] = (acc / l).astype(BF16)


def _attn_prompt(aq, iq, iw, kb, vb, ikb, nb, qb):
    rows = aq.shape[0]
    t = rows // nb
    n_sel = min(TOPK_MAX, t // 4)
    nq = t // qb
    qrow = lambda n: pl.BlockSpec((qb, n), lambda b, i: (b * nq + i, 0))
    seq = lambda n: pl.BlockSpec((t, n), lambda b, i: (b, 0))
    return pl.pallas_call(
        functools.partial(_attn_prompt_kernel, n_sel=n_sel),
        grid=(nb, nq),
        in_specs=[qrow(AT_W), qrow(IDX_W), qrow(IDX_HEADS), seq(AT_W), seq(AT_W), seq(IDX_DIM)],
        out_specs=qrow(AT_W),
        out_shape=jax.ShapeDtypeStruct((rows, AT_W), BF16),
        scratch_shapes=[pltpu.VMEM((t // LANES, qb, LANES), I32),
                        pltpu.VMEM((IDX_HEADS, qb, LANES), F32)],
        compiler_params=_cparams(("arbitrary", "arbitrary")),
        name="attn_prompt",
    )(aq, iq, iw, kb, vb, ikb)


def _attn_sample_kernel(pt_ref, aq_ref, iq_ref, iw_ref, kn_ref, vn_ref, ikn_ref, *rest,
                        n_pages, n_sel, t):
    del pt_ref
    ik_pages = rest[:n_pages]
    k_pages = rest[n_pages:2 * n_pages]
    v_pages = rest[2 * n_pages:3 * n_pages]
    o_ref, ikb_scr, kb_scr, vb_scr = rest[3 * n_pages:]
    page = k_pages[0].shape[0]
    past = n_pages * page
    total = past + page

    for p in range(n_pages):
        rows = slice(p * page, (p + 1) * page)
        ikb_scr[rows, :] = ik_pages[p][...].astype(BF16)
        kb_scr[rows, :] = k_pages[p][...].astype(BF16)
        vb_scr[rows, :] = v_pages[p][...].astype(BF16)
    tail = slice(past, total)
    fill = lambda ref: jnp.concatenate(
        [ref[...], jnp.zeros((page - t, ref.shape[1]), F32)], axis=0).astype(BF16)
    ikb_scr[tail, :] = fill(ikn_ref)
    kb_scr[tail, :] = fill(kn_ref)
    vb_scr[tail, :] = fill(vn_ref)

    kpos = lax.broadcasted_iota(I32, (1, total), 1)
    qpos = past + lax.broadcasted_iota(I32, (t, 1), 0)

    d = _dot_nt(iq_ref[...], ikb_scr[...])
    r = jnp.maximum(d, 0.0) * iw_ref[...]
    score = jnp.sum(r.reshape(t, IDX_HEADS, total), axis=1) * IDX_SCALE
    key = jnp.where(kpos <= qpos, _order_key(score), INT_MIN)

    need = jnp.minimum(qpos + 1, n_sel).astype(F32)
    count = lambda m: jnp.sum(jnp.where(m, 1.0, 0.0), axis=1, keepdims=True)
    zero = jnp.zeros((t, 1), I32)
    start = jnp.where(count(key >= zero) >= need, 0, INT_MIN).astype(I32)
    thr = _bit_search(lambda x: count(key >= x) >= need, 31, start)
    rest_n = need - count(key > thr)
    tied = key == thr
    nbits = max(1, (total - 1).bit_length())
    cut = _bit_search(lambda x: count(tied & (kpos < x)) < rest_n, nbits, zero)
    sel = (key > thr) | (tied & (kpos <= cut))

    scale = AT_DH ** -0.5
    aq = aq_ref[...].astype(BF16)
    for h in range(AT_HEADS):
        sl = slice(h * AT_DH, (h + 1) * AT_DH)
        s = jnp.where(sel, _dot_nt(aq[:, sl], kb_scr[:, sl]) * scale, NEG_BIG)
        p = jnp.exp(s - jnp.max(s, axis=1, keepdims=True))
        l = jnp.sum(p, axis=1, keepdims=True)
        o_ref[:, sl] = _dot(p.astype(BF16), vb_scr[:, sl]) / l


def _attn_sample(page_table, aq, iq, iw, kn, vn, ikn, cache_ik, cache_k, cache_v, t_real):
    nseq, n_pages = page_table.shape
    page = cache_k.shape[1]
    past = n_pages * page
    t = aq.shape[1]
    n_sel = min(TOPK_MAX, (past + t_real) // 4)
    tok = lambda n: pl.BlockSpec((None, t, n), lambda s, pt: (s, 0, 0))
    pg = lambda n, p: pl.BlockSpec((None, page, n), lambda s, pt, p=p: (pt[s, p], 0, 0))
    in_specs = [tok(AT_W),
                pl.BlockSpec((None, t * IDX_HEADS, IDX_DIM), lambda s, pt: (s, 0, 0)),
                pl.BlockSpec((None, t * IDX_HEADS, 1), lambda s, pt: (s, 0, 0)),
                tok(AT_W), tok(AT_W), tok(IDX_DIM)]
    in_specs += [pg(IDX_DIM, p) for p in range(n_pages)]
    in_specs += [pg(AT_W, p) for p in range(n_pages)]
    in_specs += [pg(AT_W, p) for p in range(n_pages)]
    total = past + page
    return pl.pallas_call(
        functools.partial(_attn_sample_kernel, n_pages=n_pages, n_sel=n_sel, t=t),
        grid_spec=pltpu.PrefetchScalarGridSpec(
            num_scalar_prefetch=1,
            grid=(nseq,),
            in_specs=in_specs,
            out_specs=pl.BlockSpec((None, t, AT_W), lambda s, pt: (s, 0, 0)),
            scratch_shapes=[pltpu.VMEM((total, IDX_DIM), BF16),
                            pltpu.VMEM((total, AT_W), BF16),
                            pltpu.VMEM((total, AT_W), BF16)]),
        out_shape=jax.ShapeDtypeStruct((nseq, t, AT_W), F32),
        compiler_params=_cparams(("arbitrary",)),
        name="attn_sample",
    )(page_table, aq, iq, iw, kn, vn, ikn,
      *([cache_ik] * n_pages), *([cache_k] * n_pages), *([cache_v] * n_pages))


def _out_kernel(x_ref, og_ref, ao_ref, gg_ref, wa_ref, wb_ref, wo_ref, n2_ref, wu_ref, wd_ref,
                nf_ref, y_ref):
    d = x_ref.shape[1]
    ya = _dot(og_ref[...], wa_ref[...])
    yb = _dot(ao_ref[...], wb_ref[...])
    m = _sigmoid(gg_ref[:, :d]) * ya + _sigmoid(gg_ref[:, d:]) * yb
    x1 = x_ref[...] + _dot(m.astype(BF16), wo_ref[...])
    h2 = _rms(x1, n2_ref[...]).astype(BF16)
    acc = x1
    ff = wu_ref.shape[1]
    for c in range(ff // d):
        cs = slice(c * d, (c + 1) * d)
        u = jnp.maximum(_dot(h2, wu_ref[:, cs]), 0.0)
        acc = acc + _dot((u * u).astype(BF16), wd_ref[cs, :])
    y_ref[...] = _rms(acc, nf_ref[...])


def _out_mlp(x, og, ao, gg, wa, wb, wo, n2, wu, wd, nf, tm):
    rows, d = x.shape
    row = lambda n: pl.BlockSpec((tm, n), lambda i: (i, 0))
    const = lambda a: pl.BlockSpec(a.shape, lambda i: (0, 0), pipeline_mode=pl.Buffered(1))
    return pl.pallas_call(
        _out_kernel,
        grid=(rows // tm,),
        in_specs=[row(d), row(HG_W), row(AT_W), row(2 * d), const(wa), const(wb), const(wo),
                  const(n2), const(wu), const(wd), const(nf)],
        out_specs=row(d),
        out_shape=jax.ShapeDtypeStruct((rows, d), F32),
        compiler_params=_cparams(("arbitrary",)),
        name="out_mlp",
    )(x, og, ao, gg, wa, wb, wo, n2, wu, wd, nf)


def _pack_w_in(w):
    d = w.shape[0]
    pad = lambda n: jnp.zeros((d, n), w.dtype)
    ik0 = _C_IK
    iw0 = ik0 + IDX_DIM
    gg0 = iw0 + IDX_HEADS
    return jnp.concatenate(
        [w[:, :ik0], w[:, ik0:iw0], pad(LANES - IDX_DIM), w[:, iw0:gg0], pad(LANES - IDX_HEADS),
         w[:, gg0:]], axis=1).astype(BF16)


def kernel(x_prompt, x_sample, cache_k, cache_v, cache_idx_k, state_hgrn, page_table, lb_logits,
           w_in, hg_norm, w_a, w_b, w_o, norm1, norm2, w_up, w_down, norm_f):
    depth = w_in.shape[0]
    assert depth == 1, "single-layer stack"
    nb, t, d = x_prompt.shape
    ns, ts, _ = x_sample.shape
    n_phys, page = cache_k.shape[1], cache_k.shape[2]

    w_in_p = _pack_w_in(w_in[0])
    assert w_in_p.shape[1] == _C_END
    bf = lambda a: a.astype(BF16)
    wa, wb, wo, wu, wd = bf(w_a[0]), bf(w_b[0]), bf(w_o[0]), bf(w_up[0]), bf(w_down[0])
    n1, n2, nf = norm1[0][None], norm2[0][None], norm_f[None]
    hgn = hg_norm[0][None]

    xp = x_prompt.reshape(nb * t, d)
    xs = x_sample.reshape(ns * ts, d)
    tm_p, tm_s = min(ROW_TILE, nb * t), min(ROW_TILE, ns * ts)
    zh_p, aq_p, ak_p, av_p, akb_p, avb_p, iq_p, ik_p, ikb_p, iw_p, gg_p = _inproj(xp, n1, w_in_p, tm_p)
    zh_s, aq_s, ak_s, av_s, _, _, iq_s, ik_s, _, iw_s, gg_s = _inproj(xs, n1, w_in_p, tm_s)

    og_p, st_p = _hgrn_prompt(lb_logits, hgn, zh_p.reshape(nb, t, 4 * HG_W))
    og_s, st_s = _hgrn_sample(lb_logits, hgn, zh_s, state_hgrn[0], ts, 8)

    ao_p = _attn_prompt(aq_p, iq_p, iw_p, akb_p, avb_p, ikb_p, nb, LANES)

    tp = -(-ts // 8) * 8
    tok = lambda a: jnp.pad(a.reshape(ns, ts, -1), ((0, 0), (0, tp - ts), (0, 0)))
    ao_s = _attn_sample(
        page_table,
        tok(aq_s.astype(F32)),
        tok(iq_s).reshape(ns, tp * IDX_HEADS, IDX_DIM),
        tok(iw_s).reshape(ns, tp * IDX_HEADS, 1),
        tok(ak_s), tok(av_s), tok(ik_s),
        cache_idx_k[0], cache_k[0].reshape(n_phys, page, AT_W), cache_v[0].reshape(n_phys, page, AT_W),
        ts)
    ao_s = ao_s[:, :ts].reshape(ns * ts, AT_W).astype(BF16)

    y_p = _out_mlp(xp, og_p.reshape(nb * t, HG_W), ao_p, gg_p, wa, wb, wo, n2, wu, wd, nf, tm_p)
    y_s = _out_mlp(xs, og_s, ao_s, gg_s, wa, wb, wo, n2, wu, wd, nf, tm_s)

    return (y_p.reshape(nb, t, d), y_s.reshape(ns, ts, d),
            ak_p.reshape(1, nb, t, AT_HEADS, AT_DH), av_p.reshape(1, nb, t, AT_HEADS, AT_DH),
            ik_p.reshape(1, nb, t, IDX_DIM), st_p[None],
            ak_s.reshape(1, ns, ts, AT_HEADS, AT_DH), av_s.reshape(1, ns, ts, AT_HEADS, AT_DH),
            ik_s.reshape(1, ns, ts, IDX_DIM), st_s[None])
```

```python
import functools

import jax
import jax.numpy as jnp
from jax import lax
from jax.experimental import pallas as pl
from jax.experimental.pallas import tpu as pltpu

F32 = jnp.float32
BF16 = jnp.bfloat16
I32 = jnp.int32

EPS = 1e-6
HG_HEADS = 4
HG_D = 128
HG_W = HG_HEADS * HG_D
HG_CHUNK = 64
AT_HEADS = 4
AT_DH = 128
AT_W = AT_HEADS * AT_DH
IDX_HEADS = 8
IDX_DIM = 64
IDX_W = IDX_HEADS * IDX_DIM
IDX_SCALE = (IDX_HEADS * IDX_DIM) ** -0.5
TOPK_MAX = 256
LANES = 128
INT_MIN = -(2 ** 31)
NEG_BIG = -1e30
VMEM_LIMIT = 56 * 1024 * 1024
ROW_TILE = 512
KEY_GROUP = 512

_C_ZH = 0
_C_AQ = 2048
_C_AK = 2560
_C_AV = 3072
_C_IQ = 3584
_C_IK = 4096
_C_IW = 4224
_C_GG = 4352
_C_END = 6400


def _dot(a, b, **kw):
    return jnp.dot(a, b, preferred_element_type=F32, **kw)


def _dot_nt(a, b):
    return lax.dot_general(a, b, (((1,), (1,)), ((), ())), preferred_element_type=F32)


def _dot_tn(a, b):
    return lax.dot_general(a, b, (((0,), (0,)), ((), ())), preferred_element_type=F32)


def _rms(x, g):
    return x * lax.rsqrt(jnp.mean(x * x, axis=-1, keepdims=True) + EPS) * g


def _sigmoid(x):
    return 1.0 / (1.0 + jnp.exp(-x))


def _cparams(sem):
    return pltpu.CompilerParams(dimension_semantics=sem, vmem_limit_bytes=VMEM_LIMIT)


def _inproj_kernel(x_ref, g_ref, w_ref, zh_ref, aq_ref, ak_ref, av_ref, akb_ref, avb_ref,
                   iq_ref, ik_ref, ikb_ref, iw_ref, gg_ref):
    h = _rms(x_ref[...], g_ref[...]).astype(BF16)

    def mm(a, n):
        return _dot(h, w_ref[:, a:a + n])

    for c in range(4):
        zh_ref[:, c * 512:(c + 1) * 512] = mm(_C_ZH + c * 512, 512)
    aq_ref[...] = mm(_C_AQ, AT_W).astype(BF16)
    k = mm(_C_AK, AT_W)
    ak_ref[...] = k
    akb_ref[...] = k.astype(BF16)
    v = mm(_C_AV, AT_W)
    av_ref[...] = v
    avb_ref[...] = v.astype(BF16)
    iq_ref[...] = mm(_C_IQ, IDX_W).astype(BF16)
    ik = mm(_C_IK, LANES)[:, :IDX_DIM]
    ik_ref[...] = ik
    ikb_ref[...] = ik.astype(BF16)
    iw_ref[...] = mm(_C_IW, LANES)[:, :IDX_HEADS]
    for c in range(4):
        gg_ref[:, c * 512:(c + 1) * 512] = mm(_C_GG + c * 512, 512)


def _inproj(x, g, w, tm):
    rows, d = x.shape
    row = lambda n: pl.BlockSpec((tm, n), lambda i: (i, 0))
    outs = [(2048, F32), (AT_W, BF16), (AT_W, F32), (AT_W, F32), (AT_W, BF16), (AT_W, BF16),
            (IDX_W, BF16), (IDX_DIM, F32), (IDX_DIM, BF16), (IDX_HEADS, F32), (2048, F32)]
    return pl.pallas_call(
        _inproj_kernel,
        grid=(rows // tm,),
        in_specs=[row(d), pl.BlockSpec((1, d), lambda i: (0, 0)),
                  pl.BlockSpec(w.shape, lambda i: (0, 0), pipeline_mode=pl.Buffered(1))],
        out_specs=[row(n) for n, _ in outs],
        out_shape=[jax.ShapeDtypeStruct((rows, n), t) for n, t in outs],
        compiler_params=_cparams(("arbitrary",)),
        name="inproj",
    )(x, g, w)


def _lower_bound(lbl):
    e = jnp.exp(lbl - jnp.max(lbl, axis=0, keepdims=True))
    return e[0:1] / jnp.sum(e, axis=0, keepdims=True)


def _forget(hf, lb):
    f = lb + (1.0 - lb) * _sigmoid(hf)
    return jnp.log(f), 1.0 - f


def _group_row(x, gsize, r):
    n, w = x.shape
    g = x.reshape(n // gsize, gsize, w)
    return jnp.broadcast_to(g[:, r:r + 1, :], g.shape).reshape(n, w)


def _hg_out(o, hog, hgn):
    return (_rms(o, hgn) * (hog * _sigmoid(hog))).astype(BF16)


def _hgrn_prompt_kernel(lbl_ref, hgn_ref, z_ref, o_ref, st_ref, s_scr):
    i = pl.program_id(0)
    nb = z_ref.shape[0]
    c = HG_CHUNK

    @pl.when(i == 0)
    def _():
        s_scr[...] = jnp.zeros_like(s_scr)

    lb = _lower_bound(lbl_ref[...])
    hgn = hgn_ref[...]
    ti = lax.broadcasted_iota(I32, (c, c), 0)
    si = lax.broadcasted_iota(I32, (c, c), 1)
    tril = (si <= ti).astype(F32)
    rowl = lax.broadcasted_iota(I32, (c, 1), 0)
    same = {g: (ti // g) == (si // g) for g in (32, 16, 8)}
    diag_mask = same[8] & (si <= ti)

    for bi in range(nb):
        z = z_ref[bi]
        hq, hf, hi, hog = (z[:, k * HG_W:(k + 1) * HG_W] for k in range(4))
        logf, kk = _forget(hf, lb)
        b = _dot(tril, logf, precision=lax.Precision.HIGHEST)

        qs, ks = [], []
        for m in (32, 16, 8):
            upper = (rowl % (2 * m)) >= m
            e = jnp.exp(-jnp.abs(b - _group_row(b, 2 * m, m - 1)))
            qs.append(jnp.where(upper, hq * e, 0.0).astype(BF16))
            ks.append(jnp.where(upper, 0.0, kk * e).astype(BF16))
        mid = 0.5 * (_group_row(b - logf, 8, 0) + _group_row(b, 8, 7))
        qs.append((hq * jnp.exp(b - mid)).astype(BF16))
        ks.append((kk * jnp.exp(mid - b)).astype(BF16))

        qe = (hq * jnp.exp(b)).astype(BF16)
        bend = b[c - 1:c, :]
        kdec = (kk * jnp.exp(bend - b)).astype(BF16)
        ebend = jnp.exp(bend)
        vb = hi.astype(BF16)

        for h in range(HG_HEADS):
            sl = slice(h * HG_D, (h + 1) * HG_D)
            p32, p16, p8, pd = (_dot_nt(q[:, sl], k[:, sl]) for q, k in zip(qs, ks))
            a = (p32 + jnp.where(same[32], p16, 0.0) + jnp.where(same[16], p8, 0.0)
                 + jnp.where(diag_mask, pd, 0.0))
            st = s_scr[bi, h]
            o = _dot_nt(qe[:, sl], st.astype(BF16)) + _dot(a.astype(BF16), vb[:, sl])
            st_new = st * ebend[:, sl] + _dot_tn(vb[:, sl], kdec[:, sl])
            s_scr[bi, h] = st_new
            o_ref[bi, :, sl] = _hg_out(o, hog[:, sl], hgn)

            @pl.when(i == pl.num_programs(0) - 1)
            def _():
                st_ref[bi, h] = st_new.T


def _hgrn_prompt(lbl, hgn, zh):
    nb, t, _ = zh.shape
    c = HG_CHUNK
    return pl.pallas_call(
        _hgrn_prompt_kernel,
        grid=(t // c,),
        in_specs=[pl.BlockSpec(lbl.shape, lambda i: (0, 0)),
                  pl.BlockSpec(hgn.shape, lambda i: (0, 0)),
                  pl.BlockSpec((nb, c, 4 * HG_W), lambda i: (0, i, 0))],
        out_specs=[pl.BlockSpec((nb, c, HG_W), lambda i: (0, i, 0)),
                   pl.BlockSpec((nb, HG_HEADS, HG_D, HG_D), lambda i: (0, 0, 0, 0))],
        out_shape=[jax.ShapeDtypeStruct((nb, t, HG_W), BF16),
                   jax.ShapeDtypeStruct((nb, HG_HEADS, HG_D, HG_D), F32)],
        scratch_shapes=[pltpu.VMEM((nb, HG_HEADS, HG_D, HG_D), F32)],
        compiler_params=_cparams(("arbitrary",)),
        name="hgrn_prompt",
    )(lbl, hgn, zh)


def _hgrn_sample_kernel(lbl_ref, hgn_ref, z_ref, s0_ref, o_ref, st_ref, *, t):
    rows = z_ref.shape[0]
    nseq = rows // t
    lb = _lower_bound(lbl_ref[...])
    hgn = hgn_ref[...]
    z = z_ref[...]
    hq, hf, hi, hog = (z[:, k * HG_W:(k + 1) * HG_W] for k in range(4))
    logf, kk = _forget(hf, lb)
    tl = lax.broadcasted_iota(I32, (rows, 1), 0) % t

    li = lax.broadcasted_iota(I32, (HG_W, HG_W), 0) // HG_D
    lj = lax.broadcasted_iota(I32, (HG_W, HG_W), 1) // HG_D
    head_ones = (li == lj).astype(BF16)

    shift = lambda x, n: pltpu.roll(x, n % rows, axis=0)
    dsum = jnp.zeros_like(logf)
    b = jnp.zeros_like(logf)
    suf = jnp.zeros_like(logf)
    o_intra = jnp.zeros_like(logf)
    for delta in range(t):
        ok = tl >= delta
        if delta > 0:
            dsum = dsum + shift(logf, delta - 1)
            suf = suf + jnp.where(tl + delta < t, shift(logf, -delta), 0.0)
        b = b + jnp.where(ok, shift(logf, delta), 0.0)
        term = jnp.where(ok, hq * shift(kk, delta) * jnp.exp(dsum), 0.0)
        a = _dot(term.astype(BF16), head_ones)
        o_intra = o_intra + a * shift(hi, delta)

    qe = (hq * jnp.exp(b)).astype(BF16)
    kdec = kk * jnp.exp(suf)
    vb = hi.astype(BF16)
    eb = jnp.exp(b + suf)

    seq_of_row = lax.broadcasted_iota(I32, (rows, 1), 0) // t
    for h in range(HG_HEADS):
        sl = slice(h * HG_D, (h + 1) * HG_D)
        eb_t = eb[:, sl].T
        o = o_intra[:, sl]
        for s in range(nseq):
            mine = seq_of_row == s
            s0 = s0_ref[s, h]
            o = o + jnp.where(mine, _dot(qe[:, sl], s0.astype(BF16)), 0.0)
            kd = jnp.where(mine, kdec[:, sl], 0.0).astype(BF16)
            st_ref[s, h] = s0 * eb_t[:, s * t:s * t + 1] + _dot_tn(kd, vb[:, sl])
        o_ref[:, sl] = _hg_out(o, hog[:, sl], hgn)


def _hgrn_sample(lbl, hgn, zh, s0, t, seq_blk):
    rows = zh.shape[0]
    nseq = rows // t
    rb = seq_blk * t
    return pl.pallas_call(
        functools.partial(_hgrn_sample_kernel, t=t),
        grid=(nseq // seq_blk,),
        in_specs=[pl.BlockSpec(lbl.shape, lambda i: (0, 0)),
                  pl.BlockSpec(hgn.shape, lambda i: (0, 0)),
                  pl.BlockSpec((rb, 4 * HG_W), lambda i: (i, 0)),
                  pl.BlockSpec((seq_blk, HG_HEADS, HG_D, HG_D), lambda i: (i, 0, 0, 0))],
        out_specs=[pl.BlockSpec((rb, HG_W), lambda i: (i, 0)),
                   pl.BlockSpec((seq_blk, HG_HEADS, HG_D, HG_D), lambda i: (i, 0, 0, 0))],
        out_shape=[jax.ShapeDtypeStruct((rows, HG_W), BF16),
                   jax.ShapeDtypeStruct(s0.shape, F32)],
        compiler_params=_cparams(("arbitrary",)),
        name="hgrn_sample",
    )(lbl, hgn, zh, s0)


def _key_to_float(key):
    bits = key ^ ((key >> 31) & 0x7FFFFFFF)
    return lax.bitcast_convert_type(bits, F32)


def _bit_search(accept, nbits, start):
    def body(k, x):
        cand = x | lax.shift_left(jnp.int32(1), (nbits - 1 - k).astype(I32))
        return jnp.where(accept(cand), cand, x)
    return lax.fori_loop(0, nbits, body, start)


def _any(x):
    return jnp.max(jnp.where(x, 1.0, 0.0)) > 0.5


def _select(need, count, masked_min, nbits_idx):
    shape = need.shape
    ge = lambda x: count(lambda s, kp: s >= x)
    start = jnp.where(ge(jnp.zeros(shape, F32)) >= need, 0, INT_MIN).astype(I32)
    tkey = _bit_search(lambda k: ge(_key_to_float(k)) >= need, 31, start)
    tf = _key_to_float(tkey)

    in_set = lambda lo: (lambda s, kp: (s >= tf) & (s > lo))

    def stats(lo):
        m = masked_min(in_set(lo))
        return m, count(lambda s, kp: s == m)

    droppable = lambda extra, cm: (extra > 0) & (extra >= cm)

    def drop(st):
        lo, extra, m, cm = st
        d = droppable(extra, cm)
        lo = jnp.where(d, m, lo)
        return (lo, jnp.where(d, extra - cm, extra)) + stats(lo)

    lo = jnp.full(shape, -jnp.inf, F32)
    extra = count(in_set(lo)) - need
    _, extra, m, cm = lax.while_loop(lambda st: _any(droppable(st[1], st[3])), drop,
                                     (lo, extra) + stats(lo))
    keep = cm - extra
    cut = lax.cond(
        _any(extra > 0),
        lambda: _bit_search(lambda x: count(lambda s, kp: (s == m) & (kp < x)) < keep, nbits_idx,
                            jnp.zeros(shape, I32)),
        lambda: jnp.full(shape, 2 ** nbits_idx - 1, I32))
    return m, cut


def _attn_prompt_kernel(aq_ref, iq_ref, iw_ref, kb_ref, vb_ref, ikb_ref, o_ref,
                        sc_scr, w_scr, iq_scr, *, n_sel):
    qb = aq_ref.shape[0]
    kg = sc_scr.shape[2]
    nc = kg // LANES
    i = pl.program_id(1)
    ng = ((i + 1) * qb + kg - 1) // kg
    qpos = i * qb + lax.broadcasted_iota(I32, (qb, 1), 0)
    lane = lax.broadcasted_iota(I32, (1, LANES), 1)
    kpos_of = lambda g, c: g * kg + c * LANES + lane
    rows_of = lambda g: pl.ds(pl.multiple_of(g * kg, kg), kg)

    iw = iw_ref[...]
    for h in range(IDX_HEADS):
        w_scr[h] = jnp.broadcast_to(iw[:, h:h + 1], (qb, LANES))
        iq_scr[h] = iq_ref[:, h * IDX_DIM:(h + 1) * IDX_DIM]

    def score_group(g, carry):
        ikt = ikb_ref[rows_of(g), :]
        acc = [jnp.zeros((qb, LANES), F32) for _ in range(nc)]
        for h in range(IDX_HEADS):
            d = jnp.maximum(_dot_nt(iq_scr[h], ikt), 0.0)
            w = w_scr[h]
            for c in range(nc):
                acc[c] = acc[c] + d[:, c * LANES:(c + 1) * LANES] * w
        for c in range(nc):
            sc_scr[g, :, c * LANES:(c + 1) * LANES] = jnp.where(
                kpos_of(g, c) <= qpos, acc[c] * IDX_SCALE, -jnp.inf)
        return carry

    lax.fori_loop(0, ng, score_group, 0)

    def reduce_keys(pred, init, elem, combine, lane_reduce):
        def body(g, acc):
            for c in range(nc):
                s = sc_scr[g, :, c * LANES:(c + 1) * LANES]
                acc = combine(acc, elem(pred(s, kpos_of(g, c)), s))
            return acc
        acc = lax.fori_loop(0, ng, body, jnp.full((qb, LANES), init, F32))
        return lane_reduce(acc, axis=1, keepdims=True)

    count = lambda pred: reduce_keys(pred, 0.0, lambda p, s: jnp.where(p, 1.0, 0.0),
                                     lambda a, b: a + b, jnp.sum)
    masked_min = lambda pred: reduce_keys(pred, jnp.inf, lambda p, s: jnp.where(p, s, jnp.inf),
                                          jnp.minimum, jnp.min)
    need = jnp.minimum(qpos + 1, n_sel).astype(F32)
    m, cut = _select(need, count, masked_min, max(1, (kb_ref.shape[0] - 1).bit_length()))

    def bias_group(g, carry):
        for c in range(nc):
            cs = slice(c * LANES, (c + 1) * LANES)
            s = sc_scr[g, :, cs]
            sel = (s > m) | ((s == m) & (kpos_of(g, c) <= cut))
            sc_scr[g, :, cs] = jnp.where(sel, 0.0, NEG_BIG)
        return carry

    lax.fori_loop(0, ng, bias_group, 0)

    scale = AT_DH ** -0.5
    heads = [slice(h * AT_DH, (h + 1) * AT_DH) for h in range(AT_HEADS)]

    def attend(g, carry):
        bias = sc_scr[g]
        rows = rows_of(g)
        out = []
        for sl, (mx, l, acc) in zip(heads, carry):
            s = _dot_nt(aq_ref[:, sl], kb_ref[rows, sl]) * scale + bias
            mx_new = jnp.maximum(mx, jnp.max(s, axis=1, keepdims=True))
            alpha = jnp.exp(mx - mx_new)
            p = jnp.exp(s - mx_new)
            l = alpha * l + jnp.sum(p, axis=1, keepdims=True)
            acc = alpha * acc + _dot(p.astype(BF16), vb_ref[rows, sl])
            out.append((mx_new, l, acc))
        return tuple(out)

    init = tuple((jnp.full((qb, 1), NEG_BIG, F32), jnp.zeros((qb, 1), F32),
                  jnp.zeros((qb, AT_DH), F32)) for _ in heads)
    for sl, (_, l, acc) in zip(heads, lax.fori_loop(0, ng, attend, init)):
        o_ref[:, sl] = (acc / l).astype(BF16)


def _attn_prompt(aq, iq, iw, kb, vb, ikb, nb, qb):
    rows = aq.shape[0]
    t = rows // nb
    n_sel = min(TOPK_MAX, t // 4)
    nq = t // qb
    kg = min(KEY_GROUP, t)
    assert t % kg == 0 and kg % LANES == 0 and t % qb == 0
    qrow = lambda n: pl.BlockSpec((qb, n), lambda b, i: (b * nq + i, 0))
    seq = lambda n: pl.BlockSpec((t, n), lambda b, i: (b, 0))
    return pl.pallas_call(
        functools.partial(_attn_prompt_kernel, n_sel=n_sel),
        grid=(nb, nq),
        in_specs=[qrow(AT_W), qrow(IDX_W), qrow(IDX_HEADS), seq(AT_W), seq(AT_W), seq(IDX_DIM)],
        out_specs=qrow(AT_W),
        out_shape=jax.ShapeDtypeStruct((rows, AT_W), BF16),
        scratch_shapes=[pltpu.VMEM((t // kg, qb, kg), F32),
                        pltpu.VMEM((IDX_HEADS, qb, LANES), F32),
                        pltpu.VMEM((IDX_HEADS, qb, IDX_DIM), BF16)],
        compiler_params=_cparams(("arbitrary", "arbitrary")),
        name="attn_prompt",
    )(aq, iq, iw, kb, vb, ikb)


def _attn_sample_kernel(pt_ref, aq_ref, iq_ref, iw_ref, kn_ref, vn_ref, ikn_ref, *rest,
                        n_pages, n_sel, t):
    del pt_ref
    ik_pages = rest[:n_pages]
    k_pages = rest[n_pages:2 * n_pages]
    v_pages = rest[2 * n_pages:3 * n_pages]
    o_ref, ikb_scr, kb_scr, vb_scr = rest[3 * n_pages:]
    page = k_pages[0].shape[0]
    past = n_pages * page
    total = past + page

    for p in range(n_pages):
        rows = slice(p * page, (p + 1) * page)
        ikb_scr[rows, :] = ik_pages[p][...].astype(BF16)
        kb_scr[rows, :] = k_pages[p][...].astype(BF16)
        vb_scr[rows, :] = v_pages[p][...].astype(BF16)
    tail = slice(past, total)
    fill = lambda ref: jnp.concatenate(
        [ref[...], jnp.zeros((page - t, ref.shape[1]), F32)], axis=0).astype(BF16)
    ikb_scr[tail, :] = fill(ikn_ref)
    kb_scr[tail, :] = fill(kn_ref)
    vb_scr[tail, :] = fill(vn_ref)

    kpos = lax.broadcasted_iota(I32, (1, total), 1)
    qpos = past + lax.broadcasted_iota(I32, (t, 1), 0)

    d = _dot_nt(iq_ref[...], ikb_scr[...])
    r = jnp.maximum(d, 0.0) * iw_ref[...]
    score = jnp.sum(r.reshape(t, IDX_HEADS, total), axis=1) * IDX_SCALE
    score = jnp.where(kpos <= qpos, score, -jnp.inf)

    need = jnp.minimum(qpos + 1, n_sel).astype(F32)
    count = lambda pred: jnp.sum(jnp.where(pred(score, kpos), 1.0, 0.0), axis=1, keepdims=True)
    masked_min = lambda pred: jnp.min(jnp.where(pred(score, kpos), score, jnp.inf), axis=1,
                                      keepdims=True)
    m, cut = _select(need, count, masked_min, max(1, (total - 1).bit_length()))
    sel = (score > m) | ((score == m) & (kpos <= cut))

    scale = AT_DH ** -0.5
    aq = aq_ref[...].astype(BF16)
    for h in range(AT_HEADS):
        sl = slice(h * AT_DH, (h + 1) * AT_DH)
        s = jnp.where(sel, _dot_nt(aq[:, sl], kb_scr[:, sl]) * scale, NEG_BIG)
        p = jnp.exp(s - jnp.max(s, axis=1, keepdims=True))
        l = jnp.sum(p, axis=1, keepdims=True)
        o_ref[:, sl] = _dot(p.astype(BF16), vb_scr[:, sl]) / l


def _attn_sample(page_table, aq, iq, iw, kn, vn, ikn, cache_ik, cache_k, cache_v, t_real):
    nseq, n_pages = page_table.shape
    page = cache_k.shape[1]
    past = n_pages * page
    t = aq.shape[1]
    n_sel = min(TOPK_MAX, (past + t_real) // 4)
    tok = lambda n: pl.BlockSpec((None, t, n), lambda s, pt: (s, 0, 0))
    pg = lambda n, p: pl.BlockSpec((None, page, n), lambda s, pt, p=p: (pt[s, p], 0, 0))
    in_specs = [tok(AT_W),
                pl.BlockSpec((None, t * IDX_HEADS, IDX_DIM), lambda s, pt: (s, 0, 0)),
                pl.BlockSpec((None, t * IDX_HEADS, 1), lambda s, pt: (s, 0, 0)),
                tok(AT_W), tok(AT_W), tok(IDX_DIM)]
    in_specs += [pg(IDX_DIM, p) for p in range(n_pages)]
    in_specs += [pg(AT_W, p) for p in range(n_pages)]
    in_specs += [pg(AT_W, p) for p in range(n_pages)]
    total = past + page
    return pl.pallas_call(
        functools.partial(_attn_sample_kernel, n_pages=n_pages, n_sel=n_sel, t=t),
        grid_spec=pltpu.PrefetchScalarGridSpec(
            num_scalar_prefetch=1,
            grid=(nseq,),
            in_specs=in_specs,
            out_specs=pl.BlockSpec((None, t, AT_W), lambda s, pt: (s, 0, 0)),
            scratch_shapes=[pltpu.VMEM((total, IDX_DIM), BF16),
                            pltpu.VMEM((total, AT_W), BF16),
                            pltpu.VMEM((total, AT_W), BF16)]),
        out_shape=jax.ShapeDtypeStruct((nseq, t, AT_W), F32),
        compiler_params=_cparams(("arbitrary",)),
        name="attn_sample",
    )(page_table, aq, iq, iw, kn, vn, ikn,
      *([cache_ik] * n_pages), *([cache_k] * n_pages), *([cache_v] * n_pages))


def _out_kernel(x_ref, og_ref, ao_ref, gg_ref, wa_ref, wb_ref, wo_ref, n2_ref, wu_ref, wd_ref,
                nf_ref, y_ref):
    d = x_ref.shape[1]
    ya = _dot(og_ref[...], wa_ref[...])
    yb = _dot(ao_ref[...], wb_ref[...])
    m = _sigmoid(gg_ref[:, :d]) * ya + _sigmoid(gg_ref[:, d:]) * yb
    x1 = x_ref[...] + _dot(m.astype(BF16), wo_ref[...])
    h2 = _rms(x1, n2_ref[...]).astype(BF16)
    acc = x1
    ff = wu_ref.shape[1]
    for c in range(ff // d):
        cs = slice(c * d, (c + 1) * d)
        u = jnp.maximum(_dot(h2, wu_ref[:, cs]), 0.0)
        acc = acc + _dot((u * u).astype(BF16), wd_ref[cs, :])
    y_ref[...] = _rms(acc, nf_ref[...])


def _out_mlp(x, og, ao, gg, wa, wb, wo, n2, wu, wd, nf, tm):
    rows, d = x.shape
    row = lambda n: pl.BlockSpec((tm, n), lambda i: (i, 0))
    const = lambda a: pl.BlockSpec(a.shape, lambda i: (0, 0), pipeline_mode=pl.Buffered(1))
    return pl.pallas_call(
        _out_kernel,
        grid=(rows // tm,),
        in_specs=[row(d), row(HG_W), row(AT_W), row(2 * d), const(wa), const(wb), const(wo),
                  const(n2), const(wu), const(wd), const(nf)],
        out_specs=row(d),
        out_shape=jax.ShapeDtypeStruct((rows, d), F32),
        compiler_params=_cparams(("arbitrary",)),
        name="out_mlp",
    )(x, og, ao, gg, wa, wb, wo, n2, wu, wd, nf)


def _pack_w_in(w):
    d = w.shape[0]
    pad = lambda n: jnp.zeros((d, n), w.dtype)
    ik0 = _C_IK
    iw0 = ik0 + IDX_DIM
    gg0 = iw0 + IDX_HEADS
    return jnp.concatenate(
        [w[:, :ik0], w[:, ik0:iw0], pad(LANES - IDX_DIM), w[:, iw0:gg0], pad(LANES - IDX_HEADS),
         w[:, gg0:]], axis=1).astype(BF16)


def kernel(x_prompt, x_sample, cache_k, cache_v, cache_idx_k, state_hgrn, page_table, lb_logits,
           w_in, hg_norm, w_a, w_b, w_o, norm1, norm2, w_up, w_down, norm_f):
    depth = w_in.shape[0]
    assert depth == 1, "single-layer stack"
    nb, t, d = x_prompt.shape
    ns, ts, _ = x_sample.shape
    n_phys, page = cache_k.shape[1], cache_k.shape[2]

    w_in_p = _pack_w_in(w_in[0])
    assert w_in_p.shape[1] == _C_END
    bf = lambda a: a.astype(BF16)
    wa, wb, wo, wu, wd = bf(w_a[0]), bf(w_b[0]), bf(w_o[0]), bf(w_up[0]), bf(w_down[0])
    n1, n2, nf = norm1[0][None], norm2[0][None], norm_f[None]
    hgn = hg_norm[0][None]

    xp = x_prompt.reshape(nb * t, d)
    xs = x_sample.reshape(ns * ts, d)
    tm_p, tm_s = min(ROW_TILE, nb * t), min(ROW_TILE, ns * ts)
    zh_p, aq_p, ak_p, av_p, akb_p, avb_p, iq_p, ik_p, ikb_p, iw_p, gg_p = _inproj(xp, n1, w_in_p, tm_p)
    zh_s, aq_s, ak_s, av_s, _, _, iq_s, ik_s, _, iw_s, gg_s = _inproj(xs, n1, w_in_p, tm_s)

    og_p, st_p = _hgrn_prompt(lb_logits, hgn, zh_p.reshape(nb, t, 4 * HG_W))
    og_s, st_s = _hgrn_sample(lb_logits, hgn, zh_s, state_hgrn[0], ts, 8)

    ao_p = _attn_prompt(aq_p, iq_p, iw_p, akb_p, avb_p, ikb_p, nb, LANES)

    tp = -(-ts // 8) * 8
    tok = lambda a: jnp.pad(a.reshape(ns, ts, -1), ((0, 0), (0, tp - ts), (0, 0)))
    ao_s = _attn_sample(
        page_table,
        tok(aq_s.astype(F32)),
        tok(iq_s).reshape(ns, tp * IDX_HEADS, IDX_DIM),
        tok(iw_s).reshape(ns, tp * IDX_HEADS, 1),
        tok(ak_s), tok(av_s), tok(ik_s),
        cache_idx_k.reshape(n_phys, page, IDX_DIM), cache_k.reshape(n_phys, page, AT_W),
        cache_v.reshape(n_phys, page, AT_W),
        ts)
    ao_s = ao_s[:, :ts].reshape(ns * ts, AT_W).astype(BF16)

    y_p = _out_mlp(xp, og_p.reshape(nb * t, HG_W), ao_p, gg_p, wa, wb, wo, n2, wu, wd, nf, tm_p)
    y_s = _out_mlp(xs, og_s, ao_s, gg_s, wa, wb, wo, n2, wu, wd, nf, tm_s)

    return (y_p.reshape(nb, t, d), y_s.reshape(ns, ts, d),
            ak_p.reshape(1, nb, t, AT_HEADS, AT_DH), av_p.reshape(1, nb, t, AT_HEADS, AT_DH),
            ik_p.reshape(1, nb, t, IDX_DIM), st_p[None],
            ak_s.reshape(1, ns, ts, AT_HEADS, AT_DH), av_s.reshape(1, ns, ts, AT_HEADS, AT_DH),
            ik_s.reshape(1, ns, ts, IDX_DIM), st_s[None])
```

```python
import functools

import jax
import jax.numpy as jnp
from jax import lax
from jax.experimental import pallas as pl
from jax.experimental.pallas import tpu as pltpu

F32 = jnp.float32
BF16 = jnp.bfloat16
I32 = jnp.int32

EPS = 1e-6
HG_HEADS = 4
HG_D = 128
HG_W = HG_HEADS * HG_D
HG_CHUNK = 64
AT_HEADS = 4
AT_DH = 128
AT_W = AT_HEADS * AT_DH
IDX_HEADS = 8
IDX_DIM = 64
IDX_W = IDX_HEADS * IDX_DIM
IDX_SCALE = (IDX_HEADS * IDX_DIM) ** -0.5
TOPK_MAX = 256
LANES = 128
INT_MIN = -(2 ** 31)
NEG_BIG = -1e30
VMEM_LIMIT = 56 * 1024 * 1024
ROW_TILE = 512
KEY_GROUP = 512
ATT_UNROLL = 2

_C_ZH = 0
_C_AQ = 2048
_C_AK = 2560
_C_AV = 3072
_C_IQ = 3584
_C_IK = 4096
_C_IW = 4224
_C_GG = 4352
_C_END = 6400


def _dot(a, b, **kw):
    return jnp.dot(a, b, preferred_element_type=F32, **kw)


def _dot_nt(a, b):
    return lax.dot_general(a, b, (((1,), (1,)), ((), ())), preferred_element_type=F32)


def _dot_tn(a, b):
    return lax.dot_general(a, b, (((0,), (0,)), ((), ())), preferred_element_type=F32)


def _rms(x, g):
    return x * lax.rsqrt(jnp.mean(x * x, axis=-1, keepdims=True) + EPS) * g


def _sigmoid(x):
    return 1.0 / (1.0 + jnp.exp(-x))


def _cparams(sem):
    return pltpu.CompilerParams(dimension_semantics=sem, vmem_limit_bytes=VMEM_LIMIT)


def _inproj_kernel(x_ref, g_ref, w_ref, zh_ref, aq_ref, ak_ref, av_ref, akb_ref, avb_ref,
                   iq_ref, ik_ref, ikb_ref, iw_ref, gg_ref):
    h = _rms(x_ref[...], g_ref[...]).astype(BF16)

    def mm(a, n):
        return _dot(h, w_ref[:, a:a + n])

    for c in range(4):
        zh_ref[:, c * 512:(c + 1) * 512] = mm(_C_ZH + c * 512, 512)
    aq_ref[...] = mm(_C_AQ, AT_W).astype(BF16)
    k = mm(_C_AK, AT_W)
    ak_ref[...] = k
    akb_ref[...] = k.astype(BF16)
    v = mm(_C_AV, AT_W)
    av_ref[...] = v
    avb_ref[...] = v.astype(BF16)
    iq_ref[...] = mm(_C_IQ, IDX_W).astype(BF16)
    ik = mm(_C_IK, LANES)[:, :IDX_DIM]
    ik_ref[...] = ik
    ikb_ref[...] = ik.astype(BF16)
    iw_ref[...] = mm(_C_IW, LANES)[:, :IDX_HEADS]
    for c in range(4):
        gg_ref[:, c * 512:(c + 1) * 512] = mm(_C_GG + c * 512, 512)


def _inproj(x, g, w, tm):
    rows, d = x.shape
    row = lambda n: pl.BlockSpec((tm, n), lambda i: (i, 0))
    outs = [(2048, F32), (AT_W, BF16), (AT_W, F32), (AT_W, F32), (AT_W, BF16), (AT_W, BF16),
            (IDX_W, BF16), (IDX_DIM, F32), (IDX_DIM, BF16), (IDX_HEADS, F32), (2048, F32)]
    return pl.pallas_call(
        _inproj_kernel,
        grid=(rows // tm,),
        in_specs=[row(d), pl.BlockSpec((1, d), lambda i: (0, 0)),
                  pl.BlockSpec(w.shape, lambda i: (0, 0), pipeline_mode=pl.Buffered(1))],
        out_specs=[row(n) for n, _ in outs],
        out_shape=[jax.ShapeDtypeStruct((rows, n), t) for n, t in outs],
        compiler_params=_cparams(("arbitrary",)),
        name="inproj",
    )(x, g, w)


def _lower_bound(lbl):
    e = jnp.exp(lbl - jnp.max(lbl, axis=0, keepdims=True))
    return e[0:1] / jnp.sum(e, axis=0, keepdims=True)


def _forget(hf, lb):
    f = lb + (1.0 - lb) * _sigmoid(hf)
    return jnp.log(f), 1.0 - f


def _group_row(x, gsize, r):
    n, w = x.shape
    g = x.reshape(n // gsize, gsize, w)
    return jnp.broadcast_to(g[:, r:r + 1, :], g.shape).reshape(n, w)


def _hg_out(o, hog, hgn):
    return (_rms(o, hgn) * (hog * _sigmoid(hog))).astype(BF16)


def _hgrn_prompt_kernel(lbl_ref, hgn_ref, z_ref, o_ref, st_ref, s_scr):
    i = pl.program_id(0)
    nb = z_ref.shape[0]
    c = HG_CHUNK

    @pl.when(i == 0)
    def _():
        s_scr[...] = jnp.zeros_like(s_scr)

    lb = _lower_bound(lbl_ref[...])
    hgn = hgn_ref[...]
    ti = lax.broadcasted_iota(I32, (c, c), 0)
    si = lax.broadcasted_iota(I32, (c, c), 1)
    tril = (si <= ti).astype(F32)
    rowl = lax.broadcasted_iota(I32, (c, 1), 0)
    same = {g: (ti // g) == (si // g) for g in (32, 16, 8)}
    diag_mask = same[8] & (si <= ti)

    for bi in range(nb):
        z = z_ref[bi]
        hq, hf, hi, hog = (z[:, k * HG_W:(k + 1) * HG_W] for k in range(4))
        logf, kk = _forget(hf, lb)
        b = _dot(tril, logf, precision=lax.Precision.HIGHEST)

        qs, ks = [], []
        for m in (32, 16, 8):
            upper = (rowl % (2 * m)) >= m
            e = jnp.exp(-jnp.abs(b - _group_row(b, 2 * m, m - 1)))
            qs.append(jnp.where(upper, hq * e, 0.0).astype(BF16))
            ks.append(jnp.where(upper, 0.0, kk * e).astype(BF16))
        mid = 0.5 * (_group_row(b - logf, 8, 0) + _group_row(b, 8, 7))
        qs.append((hq * jnp.exp(b - mid)).astype(BF16))
        ks.append((kk * jnp.exp(mid - b)).astype(BF16))

        qe = (hq * jnp.exp(b)).astype(BF16)
        bend = b[c - 1:c, :]
        kdec = (kk * jnp.exp(bend - b)).astype(BF16)
        ebend = jnp.exp(bend)
        vb = hi.astype(BF16)

        for h in range(HG_HEADS):
            sl = slice(h * HG_D, (h + 1) * HG_D)
            p32, p16, p8, pd = (_dot_nt(q[:, sl], k[:, sl]) for q, k in zip(qs, ks))
            a = (p32 + jnp.where(same[32], p16, 0.0) + jnp.where(same[16], p8, 0.0)
                 + jnp.where(diag_mask, pd, 0.0))
            st = s_scr[bi, h]
            o = _dot_nt(qe[:, sl], st.astype(BF16)) + _dot(a.astype(BF16), vb[:, sl])
            st_new = st * ebend[:, sl] + _dot_tn(vb[:, sl], kdec[:, sl])
            s_scr[bi, h] = st_new
            o_ref[bi, :, sl] = _hg_out(o, hog[:, sl], hgn)

            @pl.when(i == pl.num_programs(0) - 1)
            def _():
                st_ref[bi, h] = st_new.T


def _hgrn_prompt(lbl, hgn, zh):
    nb, t, _ = zh.shape
    c = HG_CHUNK
    return pl.pallas_call(
        _hgrn_prompt_kernel,
        grid=(t // c,),
        in_specs=[pl.BlockSpec(lbl.shape, lambda i: (0, 0)),
                  pl.BlockSpec(hgn.shape, lambda i: (0, 0)),
                  pl.BlockSpec((nb, c, 4 * HG_W), lambda i: (0, i, 0))],
        out_specs=[pl.BlockSpec((nb, c, HG_W), lambda i: (0, i, 0)),
                   pl.BlockSpec((nb, HG_HEADS, HG_D, HG_D), lambda i: (0, 0, 0, 0))],
        out_shape=[jax.ShapeDtypeStruct((nb, t, HG_W), BF16),
                   jax.ShapeDtypeStruct((nb, HG_HEADS, HG_D, HG_D), F32)],
        scratch_shapes=[pltpu.VMEM((nb, HG_HEADS, HG_D, HG_D), F32)],
        compiler_params=_cparams(("arbitrary",)),
        name="hgrn_prompt",
    )(lbl, hgn, zh)


def _hgrn_sample_kernel(lbl_ref, hgn_ref, z_ref, s0_ref, o_ref, st_ref, *, t):
    rows = z_ref.shape[0]
    nseq = rows // t
    lb = _lower_bound(lbl_ref[...])
    hgn = hgn_ref[...]
    z = z_ref[...]
    hq, hf, hi, hog = (z[:, k * HG_W:(k + 1) * HG_W] for k in range(4))
    logf, kk = _forget(hf, lb)
    tl = lax.broadcasted_iota(I32, (rows, 1), 0) % t

    li = lax.broadcasted_iota(I32, (HG_W, HG_W), 0) // HG_D
    lj = lax.broadcasted_iota(I32, (HG_W, HG_W), 1) // HG_D
    head_ones = (li == lj).astype(BF16)

    shift = lambda x, n: pltpu.roll(x, n % rows, axis=0)
    dsum = jnp.zeros_like(logf)
    b = jnp.zeros_like(logf)
    suf = jnp.zeros_like(logf)
    o_intra = jnp.zeros_like(logf)
    for delta in range(t):
        ok = tl >= delta
        if delta > 0:
            dsum = dsum + shift(logf, delta - 1)
            suf = suf + jnp.where(tl + delta < t, shift(logf, -delta), 0.0)
        b = b + jnp.where(ok, shift(logf, delta), 0.0)
        term = jnp.where(ok, hq * shift(kk, delta) * jnp.exp(dsum), 0.0)
        a = _dot(term.astype(BF16), head_ones)
        o_intra = o_intra + a * shift(hi, delta)

    qe = (hq * jnp.exp(b)).astype(BF16)
    kdec = kk * jnp.exp(suf)
    vb = hi.astype(BF16)
    eb = jnp.exp(b + suf)

    seq_of_row = lax.broadcasted_iota(I32, (rows, 1), 0) // t
    for h in range(HG_HEADS):
        sl = slice(h * HG_D, (h + 1) * HG_D)
        eb_t = eb[:, sl].T
        o = o_intra[:, sl]
        for s in range(nseq):
            mine = seq_of_row == s
            s0 = s0_ref[s, h]
            o = o + jnp.where(mine, _dot(qe[:, sl], s0.astype(BF16)), 0.0)
            kd = jnp.where(mine, kdec[:, sl], 0.0).astype(BF16)
            st_ref[s, h] = s0 * eb_t[:, s * t:s * t + 1] + _dot_tn(kd, vb[:, sl])
        o_ref[:, sl] = _hg_out(o, hog[:, sl], hgn)


def _hgrn_sample(lbl, hgn, zh, s0, t, seq_blk):
    rows = zh.shape[0]
    nseq = rows // t
    rb = seq_blk * t
    return pl.pallas_call(
        functools.partial(_hgrn_sample_kernel, t=t),
        grid=(nseq // seq_blk,),
        in_specs=[pl.BlockSpec(lbl.shape, lambda i: (0, 0)),
                  pl.BlockSpec(hgn.shape, lambda i: (0, 0)),
                  pl.BlockSpec((rb, 4 * HG_W), lambda i: (i, 0)),
                  pl.BlockSpec((seq_blk, HG_HEADS, HG_D, HG_D), lambda i: (i, 0, 0, 0))],
        out_specs=[pl.BlockSpec((rb, HG_W), lambda i: (i, 0)),
                   pl.BlockSpec((seq_blk, HG_HEADS, HG_D, HG_D), lambda i: (i, 0, 0, 0))],
        out_shape=[jax.ShapeDtypeStruct((rows, HG_W), BF16),
                   jax.ShapeDtypeStruct(s0.shape, F32)],
        compiler_params=_cparams(("arbitrary",)),
        name="hgrn_sample",
    )(lbl, hgn, zh, s0)


def _key_to_float(key):
    bits = key ^ ((key >> 31) & 0x7FFFFFFF)
    return lax.bitcast_convert_type(bits, F32)


def _bit_search(accept, nbits, start):
    def body(k, x):
        cand = x | lax.shift_left(jnp.int32(1), jnp.asarray(nbits - 1 - k, I32))
        return jnp.where(accept(cand), cand, x)
    return lax.fori_loop(0, nbits, body, start)


def _any(x):
    return jnp.max(jnp.where(x, 1.0, 0.0)) > 0.5


def _select(need, count, masked_min, nbits_idx):
    shape = need.shape
    ge = lambda x: count(lambda s, kp: s >= x)
    start = jnp.where(ge(jnp.zeros(shape, F32)) >= need, 0, INT_MIN).astype(I32)
    tkey = _bit_search(lambda k: ge(_key_to_float(k)) >= need, 31, start)
    tf = _key_to_float(tkey)

    in_set = lambda lo: (lambda s, kp: (s >= tf) & (s > lo))

    def stats(lo):
        m = masked_min(in_set(lo))
        return m, count(lambda s, kp: s == m)

    droppable = lambda extra, cm: (extra > 0) & (extra >= cm)

    def drop(st):
        lo, extra, m, cm = st
        d = droppable(extra, cm)
        lo = jnp.where(d, m, lo)
        return (lo, jnp.where(d, extra - cm, extra)) + stats(lo)

    lo = jnp.full(shape, -jnp.inf, F32)
    extra = count(in_set(lo)) - need
    _, extra, m, cm = lax.while_loop(lambda st: _any(droppable(st[1], st[3])), drop,
                                     (lo, extra) + stats(lo))
    keep = cm - extra
    cut = lax.cond(
        _any(extra > 0),
        lambda: _bit_search(lambda x: count(lambda s, kp: (s == m) & (kp < x)) < keep, nbits_idx,
                            jnp.zeros(shape, I32)),
        lambda: jnp.full(shape, 2 ** nbits_idx - 1, I32))
    return m, cut


def _attn_prompt_kernel(aq_ref, iq_ref, iw_ref, kb_ref, vb_ref, ikb_ref, o_ref,
                        sc_scr, w_scr, iq_scr, lg_scr, *, n_sel):
    qb = aq_ref.shape[0]
    kg = sc_scr.shape[2]
    nc = kg // LANES
    i = pl.program_id(1)
    ng = ((i + 1) * qb + kg - 1) // kg
    qpos = i * qb + lax.broadcasted_iota(I32, (qb, 1), 0)
    lane = lax.broadcasted_iota(I32, (1, LANES), 1)
    kpos_of = lambda g, c: g * kg + c * LANES + lane
    rows_of = lambda g: pl.ds(pl.multiple_of(g * kg, kg), kg)

    iw = iw_ref[...]
    for h in range(IDX_HEADS):
        w_scr[h] = jnp.broadcast_to(iw[:, h:h + 1], (qb, LANES))
        iq_scr[h] = iq_ref[:, h * IDX_DIM:(h + 1) * IDX_DIM]

    def score_group(g, carry):
        ikt = ikb_ref[rows_of(g), :]
        acc = [jnp.zeros((qb, LANES), F32) for _ in range(nc)]
        for h in range(IDX_HEADS):
            d = jnp.maximum(_dot_nt(iq_scr[h], ikt), 0.0)
            w = w_scr[h]
            for c in range(nc):
                acc[c] = acc[c] + d[:, c * LANES:(c + 1) * LANES] * w
        for c in range(nc):
            sc_scr[g, :, c * LANES:(c + 1) * LANES] = jnp.where(
                kpos_of(g, c) <= qpos, acc[c] * IDX_SCALE, -jnp.inf)
        return carry

    lax.fori_loop(0, ng, score_group, 0)

    def reduce_keys(pred, init, elem, combine, lane_reduce):
        def body(g, acc):
            for c in range(nc):
                s = sc_scr[g, :, c * LANES:(c + 1) * LANES]
                acc = combine(acc, elem(pred(s, kpos_of(g, c)), s))
            return acc
        acc = lax.fori_loop(0, ng, body, jnp.full((qb, LANES), init, F32))
        return lane_reduce(acc, axis=1, keepdims=True)

    count = lambda pred: reduce_keys(pred, 0.0, lambda p, s: jnp.where(p, 1.0, 0.0),
                                     lambda a, b: a + b, jnp.sum)
    masked_min = lambda pred: reduce_keys(pred, jnp.inf, lambda p, s: jnp.where(p, s, jnp.inf),
                                          jnp.minimum, jnp.min)
    need = jnp.minimum(qpos + 1, n_sel).astype(F32)
    m, cut = _select(need, count, masked_min, max(1, (kb_ref.shape[0] - 1).bit_length()))

    def bias_group(g, carry):
        for c in range(nc):
            cs = slice(c * LANES, (c + 1) * LANES)
            s = sc_scr[g, :, cs]
            sel = (s > m) | ((s == m) & (kpos_of(g, c) <= cut))
            sc_scr[g, :, cs] = jnp.where(sel, 0.0, NEG_BIG)
        return carry

    lax.fori_loop(0, ng, bias_group, 0)

    scale = AT_DH ** -0.5
    heads = [slice(h * AT_DH, (h + 1) * AT_DH) for h in range(AT_HEADS)]

    def sweep(body, init):
        def trip(j, c):
            for k in range(ATT_UNROLL):
                c = body(j * ATT_UNROLL + k, c)
            return c
        c = lax.fori_loop(0, ng // ATT_UNROLL, trip, init)
        for k in range(ATT_UNROLL - 1, 0, -1):
            c = lax.cond(ng % ATT_UNROLL >= k, lambda c, k=k: body(ng - k, c), lambda c: c, c)
        return c

    def logits_group(g, mrun):
        bias = sc_scr[g]
        out = []
        for h, sl in enumerate(heads):
            t = _dot_nt(aq_ref[:, sl], kb_ref[rows_of(g), sl]) * scale + bias
            lg_scr[h, g] = t
            mt = mrun[h]
            for c in range(nc):
                mt = jnp.maximum(mt, t[:, c * LANES:(c + 1) * LANES])
            out.append(mt)
        return tuple(out)

    mrun = sweep(logits_group, tuple(jnp.full((qb, LANES), NEG_BIG, F32) for _ in heads))
    for h in range(AT_HEADS):
        w_scr[h] = jnp.broadcast_to(jnp.max(mrun[h], axis=1, keepdims=True), (qb, LANES))

    def pv_group(g, carry):
        out = []
        for h, (sl, (lsum, acc)) in enumerate(zip(heads, carry)):
            mx = w_scr[h]
            ps = [jnp.exp(lg_scr[h, g, :, c * LANES:(c + 1) * LANES] - mx) for c in range(nc)]
            for p in ps:
                lsum = lsum + p
            pb = jnp.concatenate([p.astype(BF16) for p in ps], axis=1)
            out.append((lsum, acc + _dot(pb, vb_ref[rows_of(g), sl])))
        return tuple(out)

    zeros = jnp.zeros((qb, LANES), F32)
    stats = sweep(pv_group, tuple((zeros, jnp.zeros((qb, AT_DH), F32)) for _ in heads))
    for sl, (lsum, acc) in zip(heads, stats):
        o_ref[:, sl] = (acc / jnp.sum(lsum, axis=1, keepdims=True)).astype(BF16)


def _attn_prompt(aq, iq, iw, kb, vb, ikb, nb, qb):
    rows = aq.shape[0]
    t = rows // nb
    n_sel = min(TOPK_MAX, t // 4)
    nq = t // qb
    kg = min(KEY_GROUP, t)
    assert t % kg == 0 and kg % LANES == 0 and t % qb == 0
    qrow = lambda n: pl.BlockSpec((qb, n), lambda b, i: (b * nq + i, 0))
    seq = lambda n: pl.BlockSpec((t, n), lambda b, i: (b, 0), pipeline_mode=pl.Buffered(1))
    return pl.pallas_call(
        functools.partial(_attn_prompt_kernel, n_sel=n_sel),
        grid=(nb, nq),
        in_specs=[qrow(AT_W), qrow(IDX_W), qrow(IDX_HEADS), seq(AT_W), seq(AT_W), seq(IDX_DIM)],
        out_specs=qrow(AT_W),
        out_shape=jax.ShapeDtypeStruct((rows, AT_W), BF16),
        scratch_shapes=[pltpu.VMEM((t // kg, qb, kg), F32),
                        pltpu.VMEM((IDX_HEADS, qb, LANES), F32),
                        pltpu.VMEM((IDX_HEADS, qb, IDX_DIM), BF16),
                        pltpu.VMEM((AT_HEADS, t // kg, qb, kg), F32)],
        compiler_params=_cparams(("arbitrary", "arbitrary")),
        name="attn_prompt",
    )(aq, iq, iw, kb, vb, ikb)


def _attn_sample_kernel(pt_ref, aq_ref, iq_ref, iw_ref, kn_ref, vn_ref, ikn_ref, *rest,
                        n_pages, n_sel, t):
    del pt_ref
    ik_pages = rest[:n_pages]
    k_pages = rest[n_pages:2 * n_pages]
    v_pages = rest[2 * n_pages:3 * n_pages]
    o_ref, ikb_scr, kb_scr, vb_scr = rest[3 * n_pages:]
    page = ik_pages[0].shape[0]
    past = n_pages * page
    total = past + page

    for p in range(n_pages):
        rows = slice(p * page, (p + 1) * page)
        ikb_scr[rows, :] = ik_pages[p][...].astype(BF16)
        for h in range(AT_HEADS):
            head_rows = pl.ds(h, page, stride=AT_HEADS)
            kb_scr[h, rows, :] = k_pages[p][head_rows, :].astype(BF16)
            vb_scr[h, rows, :] = v_pages[p][head_rows, :].astype(BF16)
    tail = slice(past, total)
    fill = lambda x: jnp.concatenate(
        [x, jnp.zeros((page - t, x.shape[1]), F32)], axis=0).astype(BF16)
    ikb_scr[tail, :] = fill(ikn_ref[...])
    for h in range(AT_HEADS):
        sl = slice(h * AT_DH, (h + 1) * AT_DH)
        kb_scr[h, tail, :] = fill(kn_ref[:, sl])
        vb_scr[h, tail, :] = fill(vn_ref[:, sl])

    kpos = lax.broadcasted_iota(I32, (1, total), 1)
    qpos = past + lax.broadcasted_iota(I32, (t, 1), 0)

    d = _dot_nt(iq_ref[...], ikb_scr[...])
    r = jnp.maximum(d, 0.0) * iw_ref[...]
    score = jnp.sum(r.reshape(t, IDX_HEADS, total), axis=1) * IDX_SCALE
    score = jnp.where(kpos <= qpos, score, -jnp.inf)

    need = jnp.minimum(qpos + 1, n_sel).astype(F32)
    count = lambda pred: jnp.sum(jnp.where(pred(score, kpos), 1.0, 0.0), axis=1, keepdims=True)
    masked_min = lambda pred: jnp.min(jnp.where(pred(score, kpos), score, jnp.inf), axis=1,
                                      keepdims=True)
    m, cut = _select(need, count, masked_min, max(1, (total - 1).bit_length()))
    sel = (score > m) | ((score == m) & (kpos <= cut))

    scale = AT_DH ** -0.5
    aq = aq_ref[...].astype(BF16)
    for h in range(AT_HEADS):
        sl = slice(h * AT_DH, (h + 1) * AT_DH)
        s = jnp.where(sel, _dot_nt(aq[:, sl], kb_scr[h]) * scale, NEG_BIG)
        p = jnp.exp(s - jnp.max(s, axis=1, keepdims=True))
        l = jnp.sum(p, axis=1, keepdims=True)
        o_ref[:, sl] = _dot(p.astype(BF16), vb_scr[h]) / l


def _attn_sample(page_table, aq, iq, iw, kn, vn, ikn, cache_ik, cache_k, cache_v, t_real):
    nseq, n_pages = page_table.shape
    page = cache_ik.shape[1]
    past = n_pages * page
    t = aq.shape[1]
    n_sel = min(TOPK_MAX, (past + t_real) // 4)
    tok = lambda n: pl.BlockSpec((None, t, n), lambda s, pt: (s, 0, 0))
    pg = lambda r, n, p: pl.BlockSpec((None, r, n), lambda s, pt, p=p: (pt[s, p], 0, 0))
    in_specs = [tok(AT_W),
                pl.BlockSpec((None, t * IDX_HEADS, IDX_DIM), lambda s, pt: (s, 0, 0)),
                pl.BlockSpec((None, t * IDX_HEADS, 1), lambda s, pt: (s, 0, 0)),
                tok(AT_W), tok(AT_W), tok(IDX_DIM)]
    in_specs += [pg(page, IDX_DIM, p) for p in range(n_pages)]
    in_specs += [pg(page * AT_HEADS, AT_DH, p) for p in range(n_pages)]
    in_specs += [pg(page * AT_HEADS, AT_DH, p) for p in range(n_pages)]
    total = past + page
    return pl.pallas_call(
        functools.partial(_attn_sample_kernel, n_pages=n_pages, n_sel=n_sel, t=t),
        grid_spec=pltpu.PrefetchScalarGridSpec(
            num_scalar_prefetch=1,
            grid=(nseq,),
            in_specs=in_specs,
            out_specs=pl.BlockSpec((None, t, AT_W), lambda s, pt: (s, 0, 0)),
            scratch_shapes=[pltpu.VMEM((total, IDX_DIM), BF16),
                            pltpu.VMEM((AT_HEADS, total, AT_DH), BF16),
                            pltpu.VMEM((AT_HEADS, total, AT_DH), BF16)]),
        out_shape=jax.ShapeDtypeStruct((nseq, t, AT_W), F32),
        compiler_params=_cparams(("arbitrary",)),
        name="attn_sample",
    )(page_table, aq, iq, iw, kn, vn, ikn,
      *([cache_ik] * n_pages), *([cache_k] * n_pages), *([cache_v] * n_pages))


def _out_kernel(x_ref, og_ref, ao_ref, gg_ref, wa_ref, wb_ref, wo_ref, n2_ref, wu_ref, wd_ref,
                nf_ref, y_ref):
    d = x_ref.shape[1]
    ya = _dot(og_ref[...], wa_ref[...])
    yb = _dot(ao_ref[...], wb_ref[...])
    m = _sigmoid(gg_ref[:, :d]) * ya + _sigmoid(gg_ref[:, d:]) * yb
    x1 = x_ref[...] + _dot(m.astype(BF16), wo_ref[...])
    h2 = _rms(x1, n2_ref[...]).astype(BF16)
    acc = x1
    ff = wu_ref.shape[1]
    for c in range(ff // d):
        cs = slice(c * d, (c + 1) * d)
        u = jnp.maximum(_dot(h2, wu_ref[:, cs]), 0.0)
        acc = acc + _dot((u * u).astype(BF16), wd_ref[cs, :])
    y_ref[...] = _rms(acc, nf_ref[...])


def _out_mlp(x, og, ao, gg, wa, wb, wo, n2, wu, wd, nf, tm):
    rows, d = x.shape
    row = lambda n: pl.BlockSpec((tm, n), lambda i: (i, 0))
    const = lambda a: pl.BlockSpec(a.shape, lambda i: (0, 0), pipeline_mode=pl.Buffered(1))
    return pl.pallas_call(
        _out_kernel,
        grid=(rows // tm,),
        in_specs=[row(d), row(HG_W), row(AT_W), row(2 * d), const(wa), const(wb), const(wo),
                  const(n2), const(wu), const(wd), const(nf)],
        out_specs=row(d),
        out_shape=jax.ShapeDtypeStruct((rows, d), F32),
        compiler_params=_cparams(("arbitrary",)),
        name="out_mlp",
    )(x, og, ao, gg, wa, wb, wo, n2, wu, wd, nf)


def _pack_w_in(w):
    d = w.shape[0]
    pad = lambda n: jnp.zeros((d, n), w.dtype)
    ik0 = _C_IK
    iw0 = ik0 + IDX_DIM
    gg0 = iw0 + IDX_HEADS
    return jnp.concatenate(
        [w[:, :ik0], w[:, ik0:iw0], pad(LANES - IDX_DIM), w[:, iw0:gg0], pad(LANES - IDX_HEADS),
         w[:, gg0:]], axis=1).astype(BF16)


def kernel(x_prompt, x_sample, cache_k, cache_v, cache_idx_k, state_hgrn, page_table, lb_logits,
           w_in, hg_norm, w_a, w_b, w_o, norm1, norm2, w_up, w_down, norm_f):
    depth = w_in.shape[0]
    assert depth == 1, "single-layer stack"
    nb, t, d = x_prompt.shape
    ns, ts, _ = x_sample.shape
    n_phys, page = cache_k.shape[1], cache_k.shape[2]

    w_in_p = _pack_w_in(w_in[0])
    assert w_in_p.shape[1] == _C_END
    bf = lambda a: a.astype(BF16)
    wa, wb, wo, wu, wd = bf(w_a[0]), bf(w_b[0]), bf(w_o[0]), bf(w_up[0]), bf(w_down[0])
    n1, n2, nf = norm1[0][None], norm2[0][None], norm_f[None]
    hgn = hg_norm[0][None]

    xp = x_prompt.reshape(nb * t, d)
    xs = x_sample.reshape(ns * ts, d)
    tm_p, tm_s = min(ROW_TILE, nb * t), min(ROW_TILE, ns * ts)
    zh_p, aq_p, ak_p, av_p, akb_p, avb_p, iq_p, ik_p, ikb_p, iw_p, gg_p = _inproj(xp, n1, w_in_p, tm_p)
    zh_s, aq_s, ak_s, av_s, _, _, iq_s, ik_s, _, iw_s, gg_s = _inproj(xs, n1, w_in_p, tm_s)

    og_p, st_p = _hgrn_prompt(lb_logits, hgn, zh_p.reshape(nb, t, 4 * HG_W))
    og_s, st_s = _hgrn_sample(lb_logits, hgn, zh_s, state_hgrn[0], ts, 8)

    ao_p = _attn_prompt(aq_p, iq_p, iw_p, akb_p, avb_p, ikb_p, nb, LANES)

    tp = -(-ts // 8) * 8
    tok = lambda a: jnp.pad(a.reshape(ns, ts, -1), ((0, 0), (0, tp - ts), (0, 0)))
    ao_s = _attn_sample(
        page_table,
        tok(aq_s.astype(F32)),
        tok(iq_s).reshape(ns, tp * IDX_HEADS, IDX_DIM),
        tok(iw_s).reshape(ns, tp * IDX_HEADS, 1),
        tok(ak_s), tok(av_s), tok(ik_s),
        cache_idx_k.reshape(n_phys, page, IDX_DIM),
        cache_k.reshape(n_phys, page * AT_HEADS, AT_DH), cache_v.reshape(n_phys, page * AT_HEADS, AT_DH),
        ts)
    ao_s = ao_s[:, :ts].reshape(ns * ts, AT_W).astype(BF16)

    y_p = _out_mlp(xp, og_p.reshape(nb * t, HG_W), ao_p, gg_p, wa, wb, wo, n2, wu, wd, nf, tm_p)
    y_s = _out_mlp(xs, og_s, ao_s, gg_s, wa, wb, wo, n2, wu, wd, nf, tm_s)

    return (y_p.reshape(nb, t, d), y_s.reshape(ns, ts, d),
            ak_p.reshape(1, nb, t, AT_HEADS, AT_DH), av_p.reshape(1, nb, t, AT_HEADS, AT_DH),
            ik_p.reshape(1, nb, t, IDX_DIM), st_p[None],
            ak_s.reshape(1, ns, ts, AT_HEADS, AT_DH), av_s.reshape(1, ns, ts, AT_HEADS, AT_DH),
            ik_s.reshape(1, ns, ts, IDX_DIM), st_s[None])
```

```python
import functools

import jax
import jax.numpy as jnp
from jax import lax
from jax.experimental import pallas as pl
from jax.experimental.pallas import tpu as pltpu

F32 = jnp.float32
BF16 = jnp.bfloat16
I32 = jnp.int32

EPS = 1e-6
HG_HEADS = 4
HG_D = 128
HG_W = HG_HEADS * HG_D
HG_CHUNK = 64
HG_STEP_CHUNKS = 4
AT_HEADS = 4
AT_DH = 128
AT_W = AT_HEADS * AT_DH
IDX_HEADS = 8
IDX_DIM = 64
IDX_W = IDX_HEADS * IDX_DIM
IDX_SCALE = (IDX_HEADS * IDX_DIM) ** -0.5
TOPK_MAX = 256
LANES = 128
INT_MIN = -(2 ** 31)
NEG_BIG = -1e30
VMEM_LIMIT = 56 * 1024 * 1024
ROW_TILE = 512
KEY_GROUP = 512
SELECT_SEQS = 16
ATT_UNROLL = 2

_C_ZH = 0
_C_AQ = 2048
_C_AK = 2560
_C_AV = 3072
_C_IQ = 3584
_C_IK = 4096
_C_IW = 4224
_C_GG = 4352
_C_END = 6400


def _dot(a, b, **kw):
    return jnp.dot(a, b, preferred_element_type=F32, **kw)


def _dot_nt(a, b):
    return lax.dot_general(a, b, (((1,), (1,)), ((), ())), preferred_element_type=F32)


def _dot_tn(a, b):
    return lax.dot_general(a, b, (((0,), (0,)), ((), ())), preferred_element_type=F32)


def _rms(x, g):
    return x * lax.rsqrt(jnp.mean(x * x, axis=-1, keepdims=True) + EPS) * g


def _sigmoid(x):
    return 1.0 / (1.0 + jnp.exp(-x))


def _cparams(sem):
    return pltpu.CompilerParams(dimension_semantics=sem, vmem_limit_bytes=VMEM_LIMIT)


def _inproj_kernel(x_ref, g_ref, w_ref, zh_ref, aq_ref, ak_ref, av_ref, akb_ref, avb_ref,
                   iq_ref, ik_ref, ikb_ref, iw_ref, gg_ref):
    h = _rms(x_ref[...], g_ref[...]).astype(BF16)

    def mm(a, n):
        return _dot(h, w_ref[:, a:a + n])

    for c in range(4):
        zh_ref[:, c * 512:(c + 1) * 512] = mm(_C_ZH + c * 512, 512)
    aq_ref[...] = mm(_C_AQ, AT_W).astype(BF16)
    tm = x_ref.shape[0]
    for c0, f_ref, b_ref in ((_C_AK, ak_ref, akb_ref), (_C_AV, av_ref, avb_ref)):
        kv = mm(c0, AT_W)
        b_ref[...] = kv.astype(BF16)
        for hd in range(AT_HEADS):
            f_ref[pl.ds(hd, tm, stride=AT_HEADS), :] = kv[:, hd * AT_DH:(hd + 1) * AT_DH]
    iq_ref[...] = mm(_C_IQ, IDX_W).astype(BF16)
    ik = mm(_C_IK, LANES)[:, :IDX_DIM]
    ik_ref[...] = ik
    ikb_ref[...] = ik.astype(BF16)
    iw_ref[...] = mm(_C_IW, LANES)[:, :IDX_HEADS]
    for c in range(4):
        gg_ref[:, c * 512:(c + 1) * 512] = mm(_C_GG + c * 512, 512)


def _inproj(x, g, w, tm):
    rows, d = x.shape
    row = lambda n, r=1: pl.BlockSpec((tm * r, n), lambda i: (i, 0))
    outs = [(2048, F32, 1), (AT_W, BF16, 1), (AT_DH, F32, AT_HEADS), (AT_DH, F32, AT_HEADS),
            (AT_W, BF16, 1), (AT_W, BF16, 1), (IDX_W, BF16, 1), (IDX_DIM, F32, 1),
            (IDX_DIM, BF16, 1), (IDX_HEADS, F32, 1), (2048, F32, 1)]
    return pl.pallas_call(
        _inproj_kernel,
        grid=(rows // tm,),
        in_specs=[row(d), pl.BlockSpec((1, d), lambda i: (0, 0)),
                  pl.BlockSpec(w.shape, lambda i: (0, 0), pipeline_mode=pl.Buffered(1))],
        out_specs=[row(n, r) for n, _, r in outs],
        out_shape=[jax.ShapeDtypeStruct((rows * r, n), t) for n, t, r in outs],
        compiler_params=_cparams(("arbitrary",)),
        name="inproj",
    )(x, g, w)


def _lower_bound(lbl):
    e = jnp.exp(lbl - jnp.max(lbl, axis=0, keepdims=True))
    return e[0:1] / jnp.sum(e, axis=0, keepdims=True)


def _forget(hf, lb):
    f = lb + (1.0 - lb) * _sigmoid(hf)
    return jnp.log(f), 1.0 - f


def _group_row(x, gsize, r):
    n, w = x.shape
    g = x.reshape(n // gsize, gsize, w)
    return jnp.broadcast_to(g[:, r:r + 1, :], g.shape).reshape(n, w)


def _hg_out(o, hog, hgn):
    return (_rms(o, hgn) * (hog * _sigmoid(hog))).astype(BF16)


def _hgrn_prompt_kernel(lbl_ref, hgn_ref, z_ref, o_ref, st_ref, s_scr):
    i = pl.program_id(0)
    nb = z_ref.shape[0]
    c = HG_CHUNK

    @pl.when(i == 0)
    def _():
        s_scr[...] = jnp.zeros_like(s_scr)

    lb = _lower_bound(lbl_ref[...])
    hgn = hgn_ref[...]
    ti = lax.broadcasted_iota(I32, (c, c), 0)
    si = lax.broadcasted_iota(I32, (c, c), 1)
    tril = (si <= ti).astype(F32)
    rowl = lax.broadcasted_iota(I32, (c, 1), 0)
    same = {g: (ti // g) == (si // g) for g in (32, 16, 8)}
    diag_mask = same[8] & (si <= ti)

    for ci, bi in ((ci, bi) for ci in range(z_ref.shape[1] // c) for bi in range(nb)):
        rows = slice(ci * c, (ci + 1) * c)
        z = z_ref[bi, rows, :]
        hq, hf, hi, hog = (z[:, k * HG_W:(k + 1) * HG_W] for k in range(4))
        logf, kk = _forget(hf, lb)
        b = _dot(tril, logf, precision=lax.Precision.HIGHEST)

        qs, ks = [], []
        for m in (32, 16, 8):
            upper = (rowl % (2 * m)) >= m
            e = jnp.exp(-jnp.abs(b - _group_row(b, 2 * m, m - 1)))
            qs.append(jnp.where(upper, hq * e, 0.0).astype(BF16))
            ks.append(jnp.where(upper, 0.0, kk * e).astype(BF16))
        mid = 0.5 * (_group_row(b - logf, 8, 0) + _group_row(b, 8, 7))
        qs.append((hq * jnp.exp(b - mid)).astype(BF16))
        ks.append((kk * jnp.exp(mid - b)).astype(BF16))

        qe = (hq * jnp.exp(b)).astype(BF16)
        bend = b[c - 1:c, :]
        kdec = (kk * jnp.exp(bend - b)).astype(BF16)
        ebend = jnp.exp(bend)
        vb = hi.astype(BF16)

        for h in range(HG_HEADS):
            sl = slice(h * HG_D, (h + 1) * HG_D)
            p32, p16, p8, pd = (_dot_nt(q[:, sl], k[:, sl]) for q, k in zip(qs, ks))
            a = (p32 + jnp.where(same[32], p16, 0.0) + jnp.where(same[16], p8, 0.0)
                 + jnp.where(diag_mask, pd, 0.0))
            st = s_scr[bi, h]
            o = _dot_nt(qe[:, sl], st.astype(BF16)) + _dot(a.astype(BF16), vb[:, sl])
            st_new = st * ebend[:, sl] + _dot_tn(vb[:, sl], kdec[:, sl])
            s_scr[bi, h] = st_new
            o_ref[bi, rows, sl] = _hg_out(o, hog[:, sl], hgn)

    @pl.when(i == pl.num_programs(0) - 1)
    def _():
        for bi in range(nb):
            for h in range(HG_HEADS):
                st_ref[bi, h] = s_scr[bi, h].T


def _hgrn_prompt(lbl, hgn, zh):
    nb, t, _ = zh.shape
    c = min(HG_CHUNK * HG_STEP_CHUNKS, t)
    assert t % c == 0 and c % HG_CHUNK == 0
    return pl.pallas_call(
        _hgrn_prompt_kernel,
        grid=(t // c,),
        in_specs=[pl.BlockSpec(lbl.shape, lambda i: (0, 0)),
                  pl.BlockSpec(hgn.shape, lambda i: (0, 0)),
                  pl.BlockSpec((nb, c, 4 * HG_W), lambda i: (0, i, 0))],
        out_specs=[pl.BlockSpec((nb, c, HG_W), lambda i: (0, i, 0)),
                   pl.BlockSpec((nb, HG_HEADS, HG_D, HG_D), lambda i: (0, 0, 0, 0))],
        out_shape=[jax.ShapeDtypeStruct((nb, t, HG_W), BF16),
                   jax.ShapeDtypeStruct((nb, HG_HEADS, HG_D, HG_D), F32)],
        scratch_shapes=[pltpu.VMEM((nb, HG_HEADS, HG_D, HG_D), F32)],
        compiler_params=_cparams(("arbitrary",)),
        name="hgrn_prompt",
    )(lbl, hgn, zh)


def _hgrn_sample_kernel(lbl_ref, hgn_ref, z_ref, s0_ref, o_ref, st_ref, *, t):
    rows = z_ref.shape[0]
    nseq = rows // t
    lb = _lower_bound(lbl_ref[...])
    hgn = hgn_ref[...]
    z = z_ref[...]
    hq, hf, hi, hog = (z[:, k * HG_W:(k + 1) * HG_W] for k in range(4))
    logf, kk = _forget(hf, lb)
    tl = lax.broadcasted_iota(I32, (rows, 1), 0) % t

    li = lax.broadcasted_iota(I32, (HG_W, HG_W), 0) // HG_D
    lj = lax.broadcasted_iota(I32, (HG_W, HG_W), 1) // HG_D
    head_ones = (li == lj).astype(BF16)

    shift = lambda x, n: pltpu.roll(x, n % rows, axis=0)
    dsum = jnp.zeros_like(logf)
    b = jnp.zeros_like(logf)
    suf = jnp.zeros_like(logf)
    o_intra = jnp.zeros_like(logf)
    for delta in range(t):
        ok = tl >= delta
        if delta > 0:
            dsum = dsum + shift(logf, delta - 1)
            suf = suf + jnp.where(tl + delta < t, shift(logf, -delta), 0.0)
        b = b + jnp.where(ok, shift(logf, delta), 0.0)
        term = jnp.where(ok, hq * shift(kk, delta) * jnp.exp(dsum), 0.0)
        a = _dot(term.astype(BF16), head_ones)
        o_intra = o_intra + a * shift(hi, delta)

    qe = (hq * jnp.exp(b)).astype(BF16)
    kdec = kk * jnp.exp(suf)
    vb = hi.astype(BF16)
    eb = jnp.exp(b + suf)

    seq_of_row = lax.broadcasted_iota(I32, (rows, 1), 0) // t
    for h in range(HG_HEADS):
        sl = slice(h * HG_D, (h + 1) * HG_D)
        eb_t = eb[:, sl].T
        o = o_intra[:, sl]
        for s in range(nseq):
            mine = seq_of_row == s
            s0 = s0_ref[s, h]
            o = o + jnp.where(mine, _dot(qe[:, sl], s0.astype(BF16)), 0.0)
            kd = jnp.where(mine, kdec[:, sl], 0.0).astype(BF16)
            st_ref[s, h] = s0 * eb_t[:, s * t:s * t + 1] + _dot_tn(kd, vb[:, sl])
        o_ref[:, sl] = _hg_out(o, hog[:, sl], hgn)


def _hgrn_sample(lbl, hgn, zh, s0, t, seq_blk):
    rows = zh.shape[0]
    nseq = rows // t
    rb = seq_blk * t
    return pl.pallas_call(
        functools.partial(_hgrn_sample_kernel, t=t),
        grid=(nseq // seq_blk,),
        in_specs=[pl.BlockSpec(lbl.shape, lambda i: (0, 0)),
                  pl.BlockSpec(hgn.shape, lambda i: (0, 0)),
                  pl.BlockSpec((rb, 4 * HG_W), lambda i: (i, 0)),
                  pl.BlockSpec((seq_blk, HG_HEADS, HG_D, HG_D), lambda i: (i, 0, 0, 0))],
        out_specs=[pl.BlockSpec((rb, HG_W), lambda i: (i, 0)),
                   pl.BlockSpec((seq_blk, HG_HEADS, HG_D, HG_D), lambda i: (i, 0, 0, 0))],
        out_shape=[jax.ShapeDtypeStruct((rows, HG_W), BF16),
                   jax.ShapeDtypeStruct(s0.shape, F32)],
        compiler_params=_cparams(("arbitrary",)),
        name="hgrn_sample",
    )(lbl, hgn, zh, s0)


def _key_to_float(key):
    bits = key ^ ((key >> 31) & 0x7FFFFFFF)
    return lax.bitcast_convert_type(bits, F32)


def _bit_search(accept, nbits, start):
    def body(k, x):
        cand = x | lax.shift_left(jnp.int32(1), jnp.asarray(nbits - 1 - k, I32))
        return jnp.where(accept(cand), cand, x)
    return lax.fori_loop(0, nbits, body, start)


def _any(x):
    return jnp.max(jnp.where(x, 1.0, 0.0)) > 0.5


def _select(need, count, masked_min, nbits_idx):
    shape = need.shape
    ge = lambda x: count(lambda s, kp: s >= x)
    start = jnp.where(ge(jnp.zeros(shape, F32)) >= need, 0, INT_MIN).astype(I32)
    tkey = _bit_search(lambda k: ge(_key_to_float(k)) >= need, 31, start)
    tf = _key_to_float(tkey)

    in_set = lambda lo: (lambda s, kp: (s >= tf) & (s > lo))

    def stats(lo):
        m = masked_min(in_set(lo))
        return m, count(lambda s, kp: s == m)

    droppable = lambda extra, cm: (extra > 0) & (extra >= cm)

    def drop(st):
        lo, extra, m, cm = st
        d = droppable(extra, cm)
        lo = jnp.where(d, m, lo)
        return (lo, jnp.where(d, extra - cm, extra)) + stats(lo)

    lo = jnp.full(shape, -jnp.inf, F32)
    extra = count(in_set(lo)) - need
    _, extra, m, cm = lax.while_loop(lambda st: _any(droppable(st[1], st[3])), drop,
                                     (lo, extra) + stats(lo))
    keep = cm - extra
    cut = lax.cond(
        _any(extra > 0),
        lambda: _bit_search(lambda x: count(lambda s, kp: (s == m) & (kp < x)) < keep, nbits_idx,
                            jnp.zeros(shape, I32)),
        lambda: jnp.full(shape, 2 ** nbits_idx - 1, I32))
    return m, cut


def _attn_prompt_kernel(aq_ref, iq_ref, iw_ref, kb_ref, vb_ref, ikb_ref, o_ref,
                        sc_scr, w_scr, iq_scr, lg_scr, *, n_sel):
    qb = aq_ref.shape[0]
    kg = sc_scr.shape[2]
    nc = kg // LANES
    i = pl.program_id(1)
    ng = ((i + 1) * qb + kg - 1) // kg
    qpos = i * qb + lax.broadcasted_iota(I32, (qb, 1), 0)
    lane = lax.broadcasted_iota(I32, (1, LANES), 1)
    kpos_of = lambda g, c: g * kg + c * LANES + lane
    rows_of = lambda g: pl.ds(pl.multiple_of(g * kg, kg), kg)

    iw = iw_ref[...]
    for h in range(IDX_HEADS):
        w_scr[h] = jnp.broadcast_to(iw[:, h:h + 1], (qb, LANES))
        iq_scr[h] = iq_ref[:, h * IDX_DIM:(h + 1) * IDX_DIM]

    def score_group(g, carry):
        ikt = ikb_ref[rows_of(g), :]
        acc = [jnp.zeros((qb, LANES), F32) for _ in range(nc)]
        for h in range(IDX_HEADS):
            d = jnp.maximum(_dot_nt(iq_scr[h], ikt), 0.0)
            w = w_scr[h]
            for c in range(nc):
                acc[c] = acc[c] + d[:, c * LANES:(c + 1) * LANES] * w
        for c in range(nc):
            sc_scr[g, :, c * LANES:(c + 1) * LANES] = jnp.where(
                kpos_of(g, c) <= qpos, acc[c] * IDX_SCALE, -jnp.inf)
        return carry

    lax.fori_loop(0, ng, score_group, 0)

    def reduce_keys(pred, init, elem, combine, lane_reduce):
        def body(g, acc):
            for c in range(nc):
                s = sc_scr[g, :, c * LANES:(c + 1) * LANES]
                acc = combine(acc, elem(pred(s, kpos_of(g, c)), s))
            return acc
        acc = lax.fori_loop(0, ng, body, jnp.full((qb, LANES), init, F32))
        return lane_reduce(acc, axis=1, keepdims=True)

    count = lambda pred: reduce_keys(pred, 0.0, lambda p, s: jnp.where(p, 1.0, 0.0),
                                     lambda a, b: a + b, jnp.sum)
    masked_min = lambda pred: reduce_keys(pred, jnp.inf, lambda p, s: jnp.where(p, s, jnp.inf),
                                          jnp.minimum, jnp.min)
    need = jnp.minimum(qpos + 1, n_sel).astype(F32)
    m, cut = _select(need, count, masked_min, max(1, (kb_ref.shape[0] - 1).bit_length()))

    def bias_group(g, carry):
        for c in range(nc):
            cs = slice(c * LANES, (c + 1) * LANES)
            s = sc_scr[g, :, cs]
            sel = (s > m) | ((s == m) & (kpos_of(g, c) <= cut))
            sc_scr[g, :, cs] = jnp.where(sel, 0.0, NEG_BIG)
        return carry

    lax.fori_loop(0, ng, bias_group, 0)

    scale = AT_DH ** -0.5
    heads = [slice(h * AT_DH, (h + 1) * AT_DH) for h in range(AT_HEADS)]

    def sweep(body, init):
        def trip(j, c):
            for k in range(ATT_UNROLL):
                c = body(j * ATT_UNROLL + k, c)
            return c
        c = lax.fori_loop(0, ng // ATT_UNROLL, trip, init)
        for k in range(ATT_UNROLL - 1, 0, -1):
            c = lax.cond(ng % ATT_UNROLL >= k, lambda c, k=k: body(ng - k, c), lambda c: c, c)
        return c

    def logits_group(g, mrun):
        bias = sc_scr[g]
        out = []
        for h, sl in enumerate(heads):
            t = _dot_nt(aq_ref[:, sl], kb_ref[rows_of(g), sl]) * scale + bias
            lg_scr[h, g] = t
            mt = mrun[h]
            for c in range(nc):
                mt = jnp.maximum(mt, t[:, c * LANES:(c + 1) * LANES])
            out.append(mt)
        return tuple(out)

    mrun = sweep(logits_group, tuple(jnp.full((qb, LANES), NEG_BIG, F32) for _ in heads))
    for h in range(AT_HEADS):
        w_scr[h] = jnp.broadcast_to(jnp.max(mrun[h], axis=1, keepdims=True), (qb, LANES))

    def pv_group(g, carry):
        out = []
        for h, (sl, (lsum, acc)) in enumerate(zip(heads, carry)):
            mx = w_scr[h]
            ps = [jnp.exp(lg_scr[h, g, :, c * LANES:(c + 1) * LANES] - mx) for c in range(nc)]
            for p in ps:
                lsum = lsum + p
            pb = jnp.concatenate([p.astype(BF16) for p in ps], axis=1)
            out.append((lsum, acc + _dot(pb, vb_ref[rows_of(g), sl])))
        return tuple(out)

    zeros = jnp.zeros((qb, LANES), F32)
    stats = sweep(pv_group, tuple((zeros, jnp.zeros((qb, AT_DH), F32)) for _ in heads))
    for sl, (lsum, acc) in zip(heads, stats):
        o_ref[:, sl] = (acc / jnp.sum(lsum, axis=1, keepdims=True)).astype(BF16)


def _attn_prompt(aq, iq, iw, kb, vb, ikb, nb, qb):
    rows = aq.shape[0]
    t = rows // nb
    n_sel = min(TOPK_MAX, t // 4)
    nq = t // qb
    kg = min(KEY_GROUP, t)
    assert t % kg == 0 and kg % LANES == 0 and t % qb == 0
    qrow = lambda n: pl.BlockSpec((qb, n), lambda b, i: (b * nq + i, 0))
    seq = lambda n: pl.BlockSpec((t, n), lambda b, i: (b, 0), pipeline_mode=pl.Buffered(1))
    return pl.pallas_call(
        functools.partial(_attn_prompt_kernel, n_sel=n_sel),
        grid=(nb, nq),
        in_specs=[qrow(AT_W), qrow(IDX_W), qrow(IDX_HEADS), seq(AT_W), seq(AT_W), seq(IDX_DIM)],
        out_specs=qrow(AT_W),
        out_shape=jax.ShapeDtypeStruct((rows, AT_W), BF16),
        scratch_shapes=[pltpu.VMEM((t // kg, qb, kg), F32),
                        pltpu.VMEM((IDX_HEADS, qb, LANES), F32),
                        pltpu.VMEM((IDX_HEADS, qb, IDX_DIM), BF16),
                        pltpu.VMEM((AT_HEADS, t // kg, qb, kg), F32)],
        compiler_params=_cparams(("arbitrary", "arbitrary")),
        name="attn_prompt",
    )(aq, iq, iw, kb, vb, ikb)


def _pad_rows(x, rows):
    return jnp.concatenate([x, jnp.zeros((rows - x.shape[0], x.shape[1]), x.dtype)], axis=0)


def _select_sample_kernel(pt_ref, iq_ref, iw_ref, ikn_ref, *rest, n_pages, n_sel, t, seq_blk):
    del pt_ref
    ik_pages = rest[:n_pages]
    bias_ref, sc_scr = rest[n_pages:]
    page = ik_pages[0].shape[1]
    past = n_pages * page
    total = past + page
    nc = total // LANES
    slot = pl.program_id(0) % seq_blk
    rows = pl.ds(pl.multiple_of(slot * t, t), t)
    lane = lax.broadcasted_iota(I32, (1, LANES), 1)

    iq = iq_ref[...]
    w = iw_ref[...]

    def head_sum(d):
        r = jnp.maximum(d, 0.0) * w
        return jnp.sum(r.reshape(t, IDX_HEADS, d.shape[1]), axis=1) * IDX_SCALE

    for p in range(n_pages):
        sc_scr[rows, p * page:(p + 1) * page] = head_sum(_dot(iq, ik_pages[p][...].astype(BF16)))
    new = head_sum(_dot_nt(iq, _pad_rows(ikn_ref[...], page).astype(BF16)))
    tok_pos = lax.broadcasted_iota(I32, (t, 1), 0)
    sc_scr[rows, past:total] = jnp.where(lane <= tok_pos, new, -jnp.inf)

    @pl.when(slot == seq_blk - 1)
    def _():
        nrow = seq_blk * t
        qpos = past + lax.broadcasted_iota(I32, (nrow, 1), 0) % t
        kpos_of = lambda c: c * LANES + lane

        def reduce_keys(pred, init, elem, combine, lane_reduce):
            acc = jnp.full((nrow, LANES), init, F32)
            for c in range(nc):
                s = sc_scr[:, c * LANES:(c + 1) * LANES]
                acc = combine(acc, elem(pred(s, kpos_of(c)), s))
            return lane_reduce(acc, axis=1, keepdims=True)

        count = lambda pred: reduce_keys(pred, 0.0, lambda p, s: jnp.where(p, 1.0, 0.0),
                                         lambda a, b: a + b, jnp.sum)
        masked_min = lambda pred: reduce_keys(pred, jnp.inf, lambda p, s: jnp.where(p, s, jnp.inf),
                                              jnp.minimum, jnp.min)
        need = jnp.minimum(qpos + 1, n_sel).astype(F32)
        m, cut = _select(need, count, masked_min, max(1, (total - 1).bit_length()))
        for c in range(nc):
            s = sc_scr[:, c * LANES:(c + 1) * LANES]
            sel = (s > m) | ((s == m) & (kpos_of(c) <= cut))
            bias_ref[:, :, c * LANES:(c + 1) * LANES] = jnp.where(sel, 0.0, NEG_BIG).reshape(
                seq_blk, t, LANES)


def _attend_sample_kernel(pt_ref, aq_ref, bias_ref, kn_ref, vn_ref, *rest, n_pages, t):
    del pt_ref
    k_pages = rest[:n_pages]
    v_pages = rest[n_pages:2 * n_pages]
    o_ref, kb_scr, vb_scr = rest[2 * n_pages:]
    page = k_pages[0].shape[0] // AT_HEADS
    past = n_pages * page
    total = past + page

    for p in range(n_pages):
        rows = slice(p * page, (p + 1) * page)
        for h in range(AT_HEADS):
            head_rows = pl.ds(h, page, stride=AT_HEADS)
            kb_scr[h, rows, :] = k_pages[p][head_rows, :].astype(BF16)
            vb_scr[h, rows, :] = v_pages[p][head_rows, :].astype(BF16)
    tail = slice(past, total)
    for h in range(AT_HEADS):
        sl = slice(h * AT_DH, (h + 1) * AT_DH)
        kb_scr[h, tail, :] = _pad_rows(kn_ref[:, sl], page).astype(BF16)
        vb_scr[h, tail, :] = _pad_rows(vn_ref[:, sl], page).astype(BF16)

    scale = AT_DH ** -0.5
    aq = aq_ref[...].astype(BF16)
    bias = bias_ref[...]
    for h in range(AT_HEADS):
        sl = slice(h * AT_DH, (h + 1) * AT_DH)
        s = _dot_nt(aq[:, sl], kb_scr[h]) * scale + bias
        p = jnp.exp(s - jnp.max(s, axis=1, keepdims=True))
        l = jnp.sum(p, axis=1, keepdims=True)
        o_ref[:, sl] = _dot(p.astype(BF16), vb_scr[h]) / l


def _attn_sample(page_table, aq, iq, iw, kn, vn, ikn, cache_ikt, cache_k, cache_v, t_real):
    nseq, n_pages = page_table.shape
    page = cache_ikt.shape[2]
    past = n_pages * page
    total = past + page
    t = aq.shape[1]
    n_sel = min(TOPK_MAX, (past + t_real) // 4)
    seq_blk = min(SELECT_SEQS, nseq)
    assert nseq % seq_blk == 0
    tok = lambda r, n: pl.BlockSpec((None, r, n), lambda s, pt: (s, 0, 0))
    pg = lambda r, n, p: pl.BlockSpec((None, r, n), lambda s, pt, p=p: (pt[s, p], 0, 0))

    bias = pl.pallas_call(
        functools.partial(_select_sample_kernel, n_pages=n_pages, n_sel=n_sel, t=t,
                          seq_blk=seq_blk),
        grid_spec=pltpu.PrefetchScalarGridSpec(
            num_scalar_prefetch=1,
            grid=(nseq,),
            in_specs=[tok(t * IDX_HEADS, IDX_DIM), tok(t * IDX_HEADS, 1), tok(t, IDX_DIM)]
            + [pg(IDX_DIM, page, p) for p in range(n_pages)],
            out_specs=pl.BlockSpec((seq_blk, t, total), lambda s, pt: (s // seq_blk, 0, 0)),
            scratch_shapes=[pltpu.VMEM((seq_blk * t, total), F32)]),
        out_shape=jax.ShapeDtypeStruct((nseq, t, total), F32),
        compiler_params=_cparams(("arbitrary",)),
        name="select_sample",
    )(page_table, iq, iw, ikn, *([cache_ikt] * n_pages))

    return pl.pallas_call(
        functools.partial(_attend_sample_kernel, n_pages=n_pages, t=t),
        grid_spec=pltpu.PrefetchScalarGridSpec(
            num_scalar_prefetch=1,
            grid=(nseq,),
            in_specs=[tok(t, AT_W), tok(t, total), tok(t, AT_W), tok(t, AT_W)]
            + [pg(page * AT_HEADS, AT_DH, p) for p in range(n_pages)] * 2,
            out_specs=tok(t, AT_W),
            scratch_shapes=[pltpu.VMEM((AT_HEADS, total, AT_DH), BF16),
                            pltpu.VMEM((AT_HEADS, total, AT_DH), BF16)]),
        out_shape=jax.ShapeDtypeStruct((nseq, t, AT_W), F32),
        compiler_params=_cparams(("arbitrary",)),
        name="attend_sample",
    )(page_table, aq, bias, kn, vn, *([cache_k] * n_pages), *([cache_v] * n_pages))


def _out_kernel(x_ref, og_ref, ao_ref, gg_ref, wa_ref, wb_ref, wo_ref, n2_ref, wu_ref, wd_ref,
                nf_ref, y_ref):
    d = x_ref.shape[1]
    ya = _dot(og_ref[...], wa_ref[...])
    yb = _dot(ao_ref[...], wb_ref[...])
    m = _sigmoid(gg_ref[:, :d]) * ya + _sigmoid(gg_ref[:, d:]) * yb
    x1 = x_ref[...] + _dot(m.astype(BF16), wo_ref[...])
    h2 = _rms(x1, n2_ref[...]).astype(BF16)
    acc = x1
    ff = wu_ref.shape[1]
    for c in range(ff // d):
        cs = slice(c * d, (c + 1) * d)
        u = jnp.maximum(_dot(h2, wu_ref[:, cs]), 0.0)
        acc = acc + _dot((u * u).astype(BF16), wd_ref[cs, :])
    y_ref[...] = _rms(acc, nf_ref[...])


def _out_mlp(x, og, ao, gg, wa, wb, wo, n2, wu, wd, nf, tm):
    rows, d = x.shape
    row = lambda n: pl.BlockSpec((tm, n), lambda i: (i, 0))
    const = lambda a: pl.BlockSpec(a.shape, lambda i: (0, 0), pipeline_mode=pl.Buffered(1))
    return pl.pallas_call(
        _out_kernel,
        grid=(rows // tm,),
        in_specs=[row(d), row(HG_W), row(AT_W), row(2 * d), const(wa), const(wb), const(wo),
                  const(n2), const(wu), const(wd), const(nf)],
        out_specs=row(d),
        out_shape=jax.ShapeDtypeStruct((rows, d), F32),
        compiler_params=_cparams(("arbitrary",)),
        name="out_mlp",
    )(x, og, ao, gg, wa, wb, wo, n2, wu, wd, nf)


def _pack_w_in(w):
    d = w.shape[0]
    pad = lambda n: jnp.zeros((d, n), w.dtype)
    ik0 = _C_IK
    iw0 = ik0 + IDX_DIM
    gg0 = iw0 + IDX_HEADS
    return jnp.concatenate(
        [w[:, :ik0], w[:, ik0:iw0], pad(LANES - IDX_DIM), w[:, iw0:gg0], pad(LANES - IDX_HEADS),
         w[:, gg0:]], axis=1).astype(BF16)


def kernel(x_prompt, x_sample, cache_k, cache_v, cache_idx_k, state_hgrn, page_table, lb_logits,
           w_in, hg_norm, w_a, w_b, w_o, norm1, norm2, w_up, w_down, norm_f):
    depth = w_in.shape[0]
    assert depth == 1, "single-layer stack"
    nb, t, d = x_prompt.shape
    ns, ts, _ = x_sample.shape
    n_phys, page = cache_k.shape[1], cache_k.shape[2]

    w_in_p = _pack_w_in(w_in[0])
    assert w_in_p.shape[1] == _C_END
    bf = lambda a: a.astype(BF16)
    wa, wb, wo, wu, wd = bf(w_a[0]), bf(w_b[0]), bf(w_o[0]), bf(w_up[0]), bf(w_down[0])
    n1, n2, nf = norm1[0][None], norm2[0][None], norm_f[None]
    hgn = hg_norm[0][None]

    xp = x_prompt.reshape(nb * t, d)
    xs = x_sample.reshape(ns * ts, d)
    tm_p, tm_s = min(ROW_TILE, nb * t), min(ROW_TILE, ns * ts)
    zh_p, aq_p, ak_p, av_p, akb_p, avb_p, iq_p, ik_p, ikb_p, iw_p, gg_p = _inproj(xp, n1, w_in_p, tm_p)
    zh_s, aq_s, ak_s, av_s, _, _, iq_s, ik_s, _, iw_s, gg_s = _inproj(xs, n1, w_in_p, tm_s)

    og_p, st_p = _hgrn_prompt(lb_logits, hgn, zh_p.reshape(nb, t, 4 * HG_W))
    og_s, st_s = _hgrn_sample(lb_logits, hgn, zh_s, state_hgrn[0], ts, 8)

    ao_p = _attn_prompt(aq_p, iq_p, iw_p, akb_p, avb_p, ikb_p, nb, LANES)

    tp = -(-ts // 8) * 8
    tok = lambda a: jnp.pad(a.reshape(ns, ts, -1), ((0, 0), (0, tp - ts), (0, 0)))
    ao_s = _attn_sample(
        page_table,
        tok(aq_s.astype(F32)),
        tok(iq_s).reshape(ns, tp * IDX_HEADS, IDX_DIM),
        tok(iw_s).reshape(ns, tp * IDX_HEADS, 1),
        tok(ak_s), tok(av_s), tok(ik_s),
        jnp.swapaxes(cache_idx_k, 2, 3).reshape(n_phys, IDX_DIM, page),
        cache_k.reshape(n_phys, page * AT_HEADS, AT_DH), cache_v.reshape(n_phys, page * AT_HEADS, AT_DH),
        ts)
    ao_s = ao_s[:, :ts].reshape(ns * ts, AT_W).astype(BF16)

    y_p = _out_mlp(xp, og_p.reshape(nb * t, HG_W), ao_p, gg_p, wa, wb, wo, n2, wu, wd, nf, tm_p)
    y_s = _out_mlp(xs, og_s, ao_s, gg_s, wa, wb, wo, n2, wu, wd, nf, tm_s)

    return (y_p.reshape(nb, t, d), y_s.reshape(ns, ts, d),
            ak_p.reshape(1, nb, t, AT_HEADS, AT_DH), av_p.reshape(1, nb, t, AT_HEADS, AT_DH),
            ik_p.reshape(1, nb, t, IDX_DIM), st_p[None],
            ak_s.reshape(1, ns, ts, AT_HEADS, AT_DH), av_s.reshape(1, ns, ts, AT_HEADS, AT_DH),
            ik_s.reshape(1, ns, ts, IDX_DIM), st_s[None])
```

```python
import functools

import jax
import jax.numpy as jnp
from jax import lax
from jax.experimental import pallas as pl
from jax.experimental.pallas import tpu as pltpu

F32 = jnp.float32
BF16 = jnp.bfloat16
I32 = jnp.int32

EPS = 1e-6
HG_HEADS = 4
HG_D = 128
HG_W = HG_HEADS * HG_D
HG_CHUNK = 64
HG_STEP_CHUNKS = 4
AT_HEADS = 4
AT_DH = 128
AT_W = AT_HEADS * AT_DH
IDX_HEADS = 8
IDX_DIM = 64
IDX_W = IDX_HEADS * IDX_DIM
IDX_SCALE = (IDX_HEADS * IDX_DIM) ** -0.5
TOPK_MAX = 256
LANES = 128
INT_MIN = -(2 ** 31)
NEG_BIG = -1e30
VMEM_LIMIT = 56 * 1024 * 1024
ROW_TILE = 512
KEY_GROUP = 512
SELECT_SEQS = 16
SELECT_ROWS = 256
SUB_ROWS = 128
ATT_UNROLL = 4

_C_ZH = 0
_C_AQ = 2048
_C_AK = 2560
_C_AV = 3072
_C_IQ = 3584
_C_IK = 4096
_C_IW = 4224
_C_GG = 4352
_C_END = 6400


def _dot(a, b, **kw):
    return jnp.dot(a, b, preferred_element_type=F32, **kw)


def _dot_nt(a, b):
    return lax.dot_general(a, b, (((1,), (1,)), ((), ())), preferred_element_type=F32)


def _dot_tn(a, b):
    return lax.dot_general(a, b, (((0,), (0,)), ((), ())), preferred_element_type=F32)


def _rms(x, g):
    return x * lax.rsqrt(jnp.mean(x * x, axis=-1, keepdims=True) + EPS) * g


def _sigmoid(x):
    return 1.0 / (1.0 + jnp.exp(-x))


def _cparams(sem):
    return pltpu.CompilerParams(dimension_semantics=sem, vmem_limit_bytes=VMEM_LIMIT)


def _inproj_kernel(x_ref, g_ref, w_ref, zh_ref, aq_ref, ak_ref, av_ref, akb_ref, avb_ref,
                   iq_ref, ik_ref, ikb_ref, iw_ref, gg_ref):
    h = _rms(x_ref[...], g_ref[...]).astype(BF16)

    def mm(a, n):
        return _dot(h, w_ref[:, a:a + n])

    for c in range(4):
        zh_ref[:, c * 512:(c + 1) * 512] = mm(_C_ZH + c * 512, 512)
    aq_ref[...] = mm(_C_AQ, AT_W).astype(BF16)
    tm = x_ref.shape[0]
    for c0, f_ref, b_ref in ((_C_AK, ak_ref, akb_ref), (_C_AV, av_ref, avb_ref)):
        kv = mm(c0, AT_W)
        b_ref[...] = kv.astype(BF16)
        for hd in range(AT_HEADS):
            f_ref[pl.ds(hd, tm, stride=AT_HEADS), :] = kv[:, hd * AT_DH:(hd + 1) * AT_DH]
    iq_ref[...] = mm(_C_IQ, IDX_W).astype(BF16)
    ik = mm(_C_IK, LANES)[:, :IDX_DIM]
    ik_ref[...] = ik
    ikb_ref[...] = ik.astype(BF16)
    iw_ref[...] = mm(_C_IW, LANES)[:, :IDX_HEADS]
    for c in range(4):
        gg_ref[:, c * 512:(c + 1) * 512] = mm(_C_GG + c * 512, 512)


def _inproj(x, g, w, tm):
    rows, d = x.shape
    row = lambda n, r=1: pl.BlockSpec((tm * r, n), lambda i: (i, 0))
    outs = [(2048, F32, 1), (AT_W, BF16, 1), (AT_DH, F32, AT_HEADS), (AT_DH, F32, AT_HEADS),
            (AT_W, BF16, 1), (AT_W, BF16, 1), (IDX_W, BF16, 1), (IDX_DIM, F32, 1),
            (IDX_DIM, BF16, 1), (IDX_HEADS, F32, 1), (2048, F32, 1)]
    return pl.pallas_call(
        _inproj_kernel,
        grid=(rows // tm,),
        in_specs=[row(d), pl.BlockSpec((1, d), lambda i: (0, 0)),
                  pl.BlockSpec(w.shape, lambda i: (0, 0), pipeline_mode=pl.Buffered(1))],
        out_specs=[row(n, r) for n, _, r in outs],
        out_shape=[jax.ShapeDtypeStruct((rows * r, n), t) for n, t, r in outs],
        compiler_params=_cparams(("arbitrary",)),
        name="inproj",
    )(x, g, w)


def _lower_bound(lbl):
    e = jnp.exp(lbl - jnp.max(lbl, axis=0, keepdims=True))
    return e[0:1] / jnp.sum(e, axis=0, keepdims=True)


def _forget(hf, lb):
    f = lb + (1.0 - lb) * _sigmoid(hf)
    return jnp.log(f), 1.0 - f


def _group_row(x, gsize, r):
    n, w = x.shape
    g = x.reshape(n // gsize, gsize, w)
    return jnp.broadcast_to(g[:, r:r + 1, :], g.shape).reshape(n, w)


def _hg_out(o, hog, hgn):
    return (_rms(o, hgn) * (hog * _sigmoid(hog))).astype(BF16)


def _hgrn_prompt_kernel(lbl_ref, hgn_ref, z_ref, o_ref, st_ref, s_scr):
    i = pl.program_id(0)
    nb = z_ref.shape[0]
    c = HG_CHUNK

    @pl.when(i == 0)
    def _():
        s_scr[...] = jnp.zeros_like(s_scr)

    lb = _lower_bound(lbl_ref[...])
    hgn = hgn_ref[...]
    ti = lax.broadcasted_iota(I32, (c, c), 0)
    si = lax.broadcasted_iota(I32, (c, c), 1)
    tril = (si <= ti).astype(F32)
    rowl = lax.broadcasted_iota(I32, (c, 1), 0)
    same = {g: (ti // g) == (si // g) for g in (32, 16, 8)}
    diag_mask = same[8] & (si <= ti)

    for ci, bi in ((ci, bi) for ci in range(z_ref.shape[1] // c) for bi in range(nb)):
        rows = slice(ci * c, (ci + 1) * c)
        z = z_ref[bi, rows, :]
        hq, hf, hi, hog = (z[:, k * HG_W:(k + 1) * HG_W] for k in range(4))
        logf, kk = _forget(hf, lb)
        b = _dot(tril, logf, precision=lax.Precision.HIGHEST)

        qs, ks = [], []
        for m in (32, 16, 8):
            upper = (rowl % (2 * m)) >= m
            e = jnp.exp(-jnp.abs(b - _group_row(b, 2 * m, m - 1)))
            qs.append(jnp.where(upper, hq * e, 0.0).astype(BF16))
            ks.append(jnp.where(upper, 0.0, kk * e).astype(BF16))
        mid = 0.5 * (_group_row(b - logf, 8, 0) + _group_row(b, 8, 7))
        qs.append((hq * jnp.exp(b - mid)).astype(BF16))
        ks.append((kk * jnp.exp(mid - b)).astype(BF16))

        qe = (hq * jnp.exp(b)).astype(BF16)
        bend = b[c - 1:c, :]
        kdec = (kk * jnp.exp(bend - b)).astype(BF16)
        ebend = jnp.exp(bend)
        vb = hi.astype(BF16)

        for h in range(HG_HEADS):
            sl = slice(h * HG_D, (h + 1) * HG_D)
            p32, p16, p8, pd = (_dot_nt(q[:, sl], k[:, sl]) for q, k in zip(qs, ks))
            a = (p32 + jnp.where(same[32], p16, 0.0) + jnp.where(same[16], p8, 0.0)
                 + jnp.where(diag_mask, pd, 0.0))
            st = s_scr[bi, h]
            o = _dot_nt(qe[:, sl], st.astype(BF16)) + _dot(a.astype(BF16), vb[:, sl])
            st_new = st * ebend[:, sl] + _dot_tn(vb[:, sl], kdec[:, sl])
            s_scr[bi, h] = st_new
            o_ref[bi, rows, sl] = _hg_out(o, hog[:, sl], hgn)

    @pl.when(i == pl.num_programs(0) - 1)
    def _():
        for bi in range(nb):
            for h in range(HG_HEADS):
                st_ref[bi, h] = s_scr[bi, h].T


def _hgrn_prompt(lbl, hgn, zh):
    nb, t, _ = zh.shape
    c = min(HG_CHUNK * HG_STEP_CHUNKS, t)
    assert t % c == 0 and c % HG_CHUNK == 0
    return pl.pallas_call(
        _hgrn_prompt_kernel,
        grid=(t // c,),
        in_specs=[pl.BlockSpec(lbl.shape, lambda i: (0, 0)),
                  pl.BlockSpec(hgn.shape, lambda i: (0, 0)),
                  pl.BlockSpec((nb, c, 4 * HG_W), lambda i: (0, i, 0))],
        out_specs=[pl.BlockSpec((nb, c, HG_W), lambda i: (0, i, 0)),
                   pl.BlockSpec((nb, HG_HEADS, HG_D, HG_D), lambda i: (0, 0, 0, 0))],
        out_shape=[jax.ShapeDtypeStruct((nb, t, HG_W), BF16),
                   jax.ShapeDtypeStruct((nb, HG_HEADS, HG_D, HG_D), F32)],
        scratch_shapes=[pltpu.VMEM((nb, HG_HEADS, HG_D, HG_D), F32)],
        compiler_params=_cparams(("arbitrary",)),
        name="hgrn_prompt",
    )(lbl, hgn, zh)


def _hgrn_sample_kernel(lbl_ref, hgn_ref, z_ref, s0_ref, o_ref, st_ref, *, t):
    rows = z_ref.shape[0]
    nseq = rows // t
    lb = _lower_bound(lbl_ref[...])
    hgn = hgn_ref[...]
    z = z_ref[...]
    hq, hf, hi, hog = (z[:, k * HG_W:(k + 1) * HG_W] for k in range(4))
    logf, kk = _forget(hf, lb)
    tl = lax.broadcasted_iota(I32, (rows, 1), 0) % t

    li = lax.broadcasted_iota(I32, (HG_W, HG_W), 0) // HG_D
    lj = lax.broadcasted_iota(I32, (HG_W, HG_W), 1) // HG_D
    head_ones = (li == lj).astype(BF16)

    shift = lambda x, n: pltpu.roll(x, n % rows, axis=0)
    dsum = jnp.zeros_like(logf)
    b = jnp.zeros_like(logf)
    suf = jnp.zeros_like(logf)
    o_intra = jnp.zeros_like(logf)
    for delta in range(t):
        ok = tl >= delta
        if delta > 0:
            dsum = dsum + shift(logf, delta - 1)
            suf = suf + jnp.where(tl + delta < t, shift(logf, -delta), 0.0)
        b = b + jnp.where(ok, shift(logf, delta), 0.0)
        term = jnp.where(ok, hq * shift(kk, delta) * jnp.exp(dsum), 0.0)
        a = _dot(term.astype(BF16), head_ones)
        o_intra = o_intra + a * shift(hi, delta)

    qe = (hq * jnp.exp(b)).astype(BF16)
    kdec = kk * jnp.exp(suf)
    vb = hi.astype(BF16)
    eb = jnp.exp(b + suf)

    seq_of_row = lax.broadcasted_iota(I32, (rows, 1), 0) // t
    for h in range(HG_HEADS):
        sl = slice(h * HG_D, (h + 1) * HG_D)
        eb_t = eb[:, sl].T
        o = o_intra[:, sl]
        for s in range(nseq):
            mine = seq_of_row == s
            s0 = s0_ref[s, h]
            o = o + jnp.where(mine, _dot(qe[:, sl], s0.astype(BF16)), 0.0)
            kd = jnp.where(mine, kdec[:, sl], 0.0).astype(BF16)
            st_ref[s, h] = s0 * eb_t[:, s * t:s * t + 1] + _dot_tn(kd, vb[:, sl])
        o_ref[:, sl] = _hg_out(o, hog[:, sl], hgn)


def _hgrn_sample(lbl, hgn, zh, s0, t, seq_blk):
    rows = zh.shape[0]
    nseq = rows // t
    rb = seq_blk * t
    return pl.pallas_call(
        functools.partial(_hgrn_sample_kernel, t=t),
        grid=(nseq // seq_blk,),
        in_specs=[pl.BlockSpec(lbl.shape, lambda i: (0, 0)),
                  pl.BlockSpec(hgn.shape, lambda i: (0, 0)),
                  pl.BlockSpec((rb, 4 * HG_W), lambda i: (i, 0)),
                  pl.BlockSpec((seq_blk, HG_HEADS, HG_D, HG_D), lambda i: (i, 0, 0, 0))],
        out_specs=[pl.BlockSpec((rb, HG_W), lambda i: (i, 0)),
                   pl.BlockSpec((seq_blk, HG_HEADS, HG_D, HG_D), lambda i: (i, 0, 0, 0))],
        out_shape=[jax.ShapeDtypeStruct((rows, HG_W), BF16),
                   jax.ShapeDtypeStruct(s0.shape, F32)],
        compiler_params=_cparams(("arbitrary",)),
        name="hgrn_sample",
    )(lbl, hgn, zh, s0)


def _key_to_float(key):
    bits = key ^ ((key >> 31) & 0x7FFFFFFF)
    return lax.bitcast_convert_type(bits, F32)


def _bit_search(accept, nbits, start):
    def body(k, x):
        cand = x | lax.shift_left(jnp.int32(1), jnp.asarray(nbits - 1 - k, I32))
        return jnp.where(accept(cand), cand, x)
    return lax.fori_loop(0, nbits, body, start)


def _any(x):
    return jnp.max(jnp.where(x, 1.0, 0.0)) > 0.5


def _select(need, count, masked_min, nbits_idx):
    shape = need.shape
    ge = lambda x: count(lambda s, kp, x: s >= x, x)
    start = jnp.where(ge(jnp.zeros(shape, F32)) >= need, 0, INT_MIN).astype(I32)
    tkey = _bit_search(lambda k: ge(_key_to_float(k)) >= need, 31, start)
    tf = _key_to_float(tkey)

    in_set = lambda s, kp, tf, lo: (s >= tf) & (s > lo)

    def stats(lo):
        m = masked_min(in_set, tf, lo)
        return m, count(lambda s, kp, m: s == m, m)

    droppable = lambda extra, cm: (extra > 0) & (extra >= cm)

    def drop(st):
        lo, extra, m, cm = st
        d = droppable(extra, cm)
        lo = jnp.where(d, m, lo)
        return (lo, jnp.where(d, extra - cm, extra)) + stats(lo)

    lo = jnp.full(shape, -jnp.inf, F32)
    extra = count(in_set, tf, lo) - need
    _, extra, m, cm = lax.while_loop(lambda st: _any(droppable(st[1], st[3])), drop,
                                     (lo, extra) + stats(lo))
    keep = cm - extra
    ties_before = lambda x: count(lambda s, kp, m, x: (s == m) & (kp < x), m, x)
    cut = lax.cond(
        _any(extra > 0),
        lambda: _bit_search(lambda x: ties_before(x) < keep, nbits_idx, jnp.zeros(shape, I32)),
        lambda: jnp.full(shape, 2 ** nbits_idx - 1, I32))
    return m, cut


def _attn_prompt_kernel(aq_ref, iq_ref, iw_ref, kb_ref, vb_ref, ikb_ref, o_ref,
                        sc_scr, w_scr, iq_scr, lg_scr, *, n_sel):
    qb = aq_ref.shape[0]
    kg = sc_scr.shape[2]
    nc = kg // LANES
    i = pl.program_id(1)
    ng = ((i + 1) * qb + kg - 1) // kg
    qpos = i * qb + lax.broadcasted_iota(I32, (qb, 1), 0)
    lane = lax.broadcasted_iota(I32, (1, LANES), 1)
    kpos_of = lambda g, c: g * kg + c * LANES + lane
    rows_of = lambda g: pl.ds(pl.multiple_of(g * kg, kg), kg)

    iw = iw_ref[...]
    for h in range(IDX_HEADS):
        w_scr[h] = jnp.broadcast_to(iw[:, h:h + 1], (qb, LANES))
        iq_scr[h] = iq_ref[:, h * IDX_DIM:(h + 1) * IDX_DIM]

    def sweep(body, init, n):
        def trip(j, c):
            for k in range(ATT_UNROLL):
                c = body(j * ATT_UNROLL + k, c)
            return c
        c = lax.fori_loop(0, n // ATT_UNROLL, trip, init)
        for k in range(ATT_UNROLL - 1, 0, -1):
            c = lax.cond(n % ATT_UNROLL >= k, lambda c, k=k: body(n - k, c), lambda c: c, c)
        return c

    nsub = qb // SUB_ROWS
    sub_rows = [slice(u * SUB_ROWS, (u + 1) * SUB_ROWS) for u in range(nsub)]
    sub_ng = [(i * qb + (u + 1) * SUB_ROWS + kg - 1) // kg for u in range(nsub)]

    for rs, ng_u in zip(sub_rows, sub_ng):
        def score_group(g, carry, rs=rs):
            ikt = ikb_ref[rows_of(g), :]
            acc = [jnp.zeros((SUB_ROWS, LANES), F32) for _ in range(nc)]
            for h in range(IDX_HEADS):
                d = jnp.maximum(_dot_nt(iq_scr[h, rs, :], ikt), 0.0)
                w = w_scr[h, rs, :]
                for c in range(nc):
                    acc[c] = acc[c] + d[:, c * LANES:(c + 1) * LANES] * w
            for c in range(nc):
                sc_scr[g, rs, c * LANES:(c + 1) * LANES] = jnp.where(
                    kpos_of(g, c) <= qpos[rs], acc[c] * IDX_SCALE, -jnp.inf)
            return carry

        def no_keys(g, carry, rs=rs):
            sc_scr[g, rs, :] = jnp.full((SUB_ROWS, kg), -jnp.inf, F32)
            return carry

        sweep(score_group, 0, ng_u)
        lax.fori_loop(ng_u, ng, no_keys, 0)

    def reduce_keys(pred, ops, init, elem, combine, lane_reduce):
        tiled = [jnp.broadcast_to(x, (qb, LANES)) for x in ops]
        accs = []
        for rs in sub_rows:
            def body(g, acc, rs=rs, mine=[x[rs] for x in tiled]):
                for c in range(nc):
                    s = sc_scr[g, rs, c * LANES:(c + 1) * LANES]
                    acc = combine(acc, elem(pred(s, kpos_of(g, c), *mine), s))
                return acc
            accs.append(lax.fori_loop(0, ng, body, jnp.full((SUB_ROWS, LANES), init, F32)))
        return lane_reduce(jnp.concatenate(accs, axis=0), axis=1, keepdims=True)

    count = lambda pred, *ops: reduce_keys(pred, ops, 0.0, lambda p, s: jnp.where(p, 1.0, 0.0),
                                           lambda a, b: a + b, jnp.sum)
    masked_min = lambda pred, *ops: reduce_keys(
        pred, ops, jnp.inf, lambda p, s: jnp.where(p, s, jnp.inf), jnp.minimum, jnp.min)
    need = jnp.minimum(qpos + 1, n_sel).astype(F32)
    m, cut = _select(need, count, masked_min, max(1, (kb_ref.shape[0] - 1).bit_length()))

    def bias_group(g, carry):
        for c in range(nc):
            cs = slice(c * LANES, (c + 1) * LANES)
            s = sc_scr[g, :, cs]
            sel = (s > m) | ((s == m) & (kpos_of(g, c) <= cut))
            sc_scr[g, :, cs] = jnp.where(sel, 0.0, NEG_BIG)
        return carry

    lax.fori_loop(0, ng, bias_group, 0)

    scale = AT_DH ** -0.5
    heads = [slice(h * AT_DH, (h + 1) * AT_DH) for h in range(AT_HEADS)]

    for rs, ng_u in zip(sub_rows, sub_ng):
        def logits_group(g, mrun, rs=rs):
            bias = sc_scr[g, rs, :]
            out = []
            for h, sl in enumerate(heads):
                t = _dot_nt(aq_ref[rs, sl], kb_ref[rows_of(g), sl]) * scale + bias
                lg_scr[h, g] = t
                mt = mrun[h]
                for c in range(nc):
                    mt = jnp.maximum(mt, t[:, c * LANES:(c + 1) * LANES])
                out.append(mt)
            return tuple(out)

        mrun = sweep(logits_group,
                     tuple(jnp.full((SUB_ROWS, LANES), NEG_BIG, F32) for _ in heads), ng_u)
        for h in range(AT_HEADS):
            w_scr[h, rs, :] = jnp.broadcast_to(jnp.max(mrun[h], axis=1, keepdims=True),
                                               (SUB_ROWS, LANES))

        def pv_group(g, carry, rs=rs):
            out = []
            for h, (sl, (lsum, acc)) in enumerate(zip(heads, carry)):
                mx = w_scr[h, rs, :]
                ps = [jnp.exp(lg_scr[h, g, :, c * LANES:(c + 1) * LANES] - mx)
                      for c in range(nc)]
                for p in ps:
                    lsum = lsum + p
                pb = jnp.concatenate([p.astype(BF16) for p in ps], axis=1)
                out.append((lsum, acc + _dot(pb, vb_ref[rows_of(g), sl])))
            return tuple(out)

        zeros = jnp.zeros((SUB_ROWS, LANES), F32)
        stats = sweep(pv_group, tuple((zeros, jnp.zeros((SUB_ROWS, AT_DH), F32)) for _ in heads),
                      ng_u)
        for sl, (lsum, acc) in zip(heads, stats):
            o_ref[rs, sl] = (acc / jnp.sum(lsum, axis=1, keepdims=True)).astype(BF16)


def _attn_prompt(aq, iq, iw, kb, vb, ikb, nb, qb):
    rows = aq.shape[0]
    t = rows // nb
    n_sel = min(TOPK_MAX, t // 4)
    nq = t // qb
    kg = min(KEY_GROUP, t)
    assert t % kg == 0 and kg % LANES == 0 and t % qb == 0 and qb % SUB_ROWS == 0
    qrow = lambda n: pl.BlockSpec((qb, n), lambda b, i: (b * nq + i, 0))
    seq = lambda n: pl.BlockSpec((t, n), lambda b, i: (b, 0), pipeline_mode=pl.Buffered(1))
    return pl.pallas_call(
        functools.partial(_attn_prompt_kernel, n_sel=n_sel),
        grid=(nb, nq),
        in_specs=[qrow(AT_W), qrow(IDX_W), qrow(IDX_HEADS), seq(AT_W), seq(AT_W), seq(IDX_DIM)],
        out_specs=qrow(AT_W),
        out_shape=jax.ShapeDtypeStruct((rows, AT_W), BF16),
        scratch_shapes=[pltpu.VMEM((t // kg, qb, kg), F32),
                        pltpu.VMEM((IDX_HEADS, qb, LANES), F32),
                        pltpu.VMEM((IDX_HEADS, qb, IDX_DIM), BF16),
                        pltpu.VMEM((AT_HEADS, t // kg, SUB_ROWS, kg), F32)],
        compiler_params=_cparams(("arbitrary", "arbitrary")),
        name="attn_prompt",
    )(aq, iq, iw, kb, vb, ikb)


def _pad_rows(x, rows):
    return jnp.concatenate([x, jnp.zeros((rows - x.shape[0], x.shape[1]), x.dtype)], axis=0)


def _select_sample_kernel(pt_ref, iq_ref, iw_ref, ikn_ref, *rest, n_pages, n_sel, t, seq_blk):
    del pt_ref
    ik_pages = rest[:n_pages]
    bias_ref, sc_scr = rest[n_pages:]
    page = ik_pages[0].shape[1]
    past = n_pages * page
    total = past + page
    nc = total // LANES
    slot = pl.program_id(0) % seq_blk
    rows = pl.ds(pl.multiple_of(slot * t, t), t)
    lane = lax.broadcasted_iota(I32, (1, LANES), 1)

    iq = iq_ref[...]
    w = iw_ref[...]

    def head_sum(d):
        r = jnp.maximum(d, 0.0) * w
        return jnp.sum(r.reshape(t, IDX_HEADS, d.shape[1]), axis=1) * IDX_SCALE

    for p in range(n_pages):
        sc_scr[rows, p * page:(p + 1) * page] = head_sum(_dot(iq, ik_pages[p][...].astype(BF16)))
    new = head_sum(_dot_nt(iq, _pad_rows(ikn_ref[...], page).astype(BF16)))
    tok_pos = lax.broadcasted_iota(I32, (t, 1), 0)
    sc_scr[rows, past:total] = jnp.where(lane <= tok_pos, new, -jnp.inf)

    @pl.when(slot == seq_blk - 1)
    def _():
        nrow = seq_blk * t
        qpos = past + lax.broadcasted_iota(I32, (nrow, 1), 0) % t
        kpos_of = lambda c: c * LANES + lane

        def reduce_keys(pred, ops, init, elem, combine, lane_reduce):
            tiled = [jnp.broadcast_to(x, (nrow, LANES)) for x in ops]
            acc = jnp.full((nrow, LANES), init, F32)
            for c in range(nc):
                s = sc_scr[:, c * LANES:(c + 1) * LANES]
                acc = combine(acc, elem(pred(s, kpos_of(c), *tiled), s))
            return lane_reduce(acc, axis=1, keepdims=True)

        count = lambda pred, *ops: reduce_keys(
            pred, ops, 0.0, lambda p, s: jnp.where(p, 1.0, 0.0), lambda a, b: a + b, jnp.sum)
        masked_min = lambda pred, *ops: reduce_keys(
            pred, ops, jnp.inf, lambda p, s: jnp.where(p, s, jnp.inf), jnp.minimum, jnp.min)
        need = jnp.minimum(qpos + 1, n_sel).astype(F32)
        m, cut = _select(need, count, masked_min, max(1, (total - 1).bit_length()))
        for c in range(nc):
            s = sc_scr[:, c * LANES:(c + 1) * LANES]
            sel = (s > m) | ((s == m) & (kpos_of(c) <= cut))
            bias_ref[:, :, c * LANES:(c + 1) * LANES] = jnp.where(sel, 0.0, NEG_BIG).reshape(
                seq_blk, t, LANES)


def _attend_sample_kernel(pt_ref, aq_ref, bias_ref, kn_ref, vn_ref, *rest, n_pages, t):
    del pt_ref
    k_pages = rest[:n_pages]
    v_pages = rest[n_pages:2 * n_pages]
    o_ref, kb_scr, vb_scr = rest[2 * n_pages:]
    page = k_pages[0].shape[0] // AT_HEADS
    past = n_pages * page
    total = past + page

    for p in range(n_pages):
        rows = slice(p * page, (p + 1) * page)
        for h in range(AT_HEADS):
            head_rows = pl.ds(h, page, stride=AT_HEADS)
            kb_scr[h, rows, :] = k_pages[p][head_rows, :].astype(BF16)
            vb_scr[h, rows, :] = v_pages[p][head_rows, :].astype(BF16)
    tail = slice(past, total)
    for h in range(AT_HEADS):
        sl = slice(h * AT_DH, (h + 1) * AT_DH)
        kb_scr[h, tail, :] = _pad_rows(kn_ref[:, sl], page).astype(BF16)
        vb_scr[h, tail, :] = _pad_rows(vn_ref[:, sl], page).astype(BF16)

    scale = AT_DH ** -0.5
    aq = aq_ref[...].astype(BF16)
    bias = bias_ref[...]
    for h in range(AT_HEADS):
        sl = slice(h * AT_DH, (h + 1) * AT_DH)
        s = _dot_nt(aq[:, sl], kb_scr[h]) * scale + bias
        p = jnp.exp(s - jnp.max(s, axis=1, keepdims=True))
        l = jnp.sum(p, axis=1, keepdims=True)
        o_ref[:, sl] = _dot(p.astype(BF16), vb_scr[h]) / l


def _attn_sample(page_table, aq, iq, iw, kn, vn, ikn, cache_ikt, cache_k, cache_v, t_real):
    nseq, n_pages = page_table.shape
    page = cache_ikt.shape[2]
    past = n_pages * page
    total = past + page
    t = aq.shape[1]
    n_sel = min(TOPK_MAX, (past + t_real) // 4)
    seq_blk = min(SELECT_SEQS, nseq)
    assert nseq % seq_blk == 0
    tok = lambda r, n: pl.BlockSpec((None, r, n), lambda s, pt: (s, 0, 0))
    pg = lambda r, n, p: pl.BlockSpec((None, r, n), lambda s, pt, p=p: (pt[s, p], 0, 0))

    bias = pl.pallas_call(
        functools.partial(_select_sample_kernel, n_pages=n_pages, n_sel=n_sel, t=t,
                          seq_blk=seq_blk),
        grid_spec=pltpu.PrefetchScalarGridSpec(
            num_scalar_prefetch=1,
            grid=(nseq,),
            in_specs=[tok(t * IDX_HEADS, IDX_DIM), tok(t * IDX_HEADS, 1), tok(t, IDX_DIM)]
            + [pg(IDX_DIM, page, p) for p in range(n_pages)],
            out_specs=pl.BlockSpec((seq_blk, t, total), lambda s, pt: (s // seq_blk, 0, 0)),
            scratch_shapes=[pltpu.VMEM((seq_blk * t, total), F32)]),
        out_shape=jax.ShapeDtypeStruct((nseq, t, total), F32),
        compiler_params=_cparams(("arbitrary",)),
        name="select_sample",
    )(page_table, iq, iw, ikn, *([cache_ikt] * n_pages))

    return pl.pallas_call(
        functools.partial(_attend_sample_kernel, n_pages=n_pages, t=t),
        grid_spec=pltpu.PrefetchScalarGridSpec(
            num_scalar_prefetch=1,
            grid=(nseq,),
            in_specs=[tok(t, AT_W), tok(t, total), tok(t, AT_W), tok(t, AT_W)]
            + [pg(page * AT_HEADS, AT_DH, p) for p in range(n_pages)] * 2,
            out_specs=tok(t, AT_W),
            scratch_shapes=[pltpu.VMEM((AT_HEADS, total, AT_DH), BF16),
                            pltpu.VMEM((AT_HEADS, total, AT_DH), BF16)]),
        out_shape=jax.ShapeDtypeStruct((nseq, t, AT_W), F32),
        compiler_params=_cparams(("arbitrary",)),
        name="attend_sample",
    )(page_table, aq, bias, kn, vn, *([cache_k] * n_pages), *([cache_v] * n_pages))


def _out_kernel(x_ref, og_ref, ao_ref, gg_ref, wa_ref, wb_ref, wo_ref, n2_ref, wu_ref, wd_ref,
                nf_ref, y_ref):
    d = x_ref.shape[1]
    ya = _dot(og_ref[...], wa_ref[...])
    yb = _dot(ao_ref[...], wb_ref[...])
    m = _sigmoid(gg_ref[:, :d]) * ya + _sigmoid(gg_ref[:, d:]) * yb
    x1 = x_ref[...] + _dot(m.astype(BF16), wo_ref[...])
    h2 = _rms(x1, n2_ref[...]).astype(BF16)
    acc = x1
    ff = wu_ref.shape[1]
    for c in range(ff // d):
        cs = slice(c * d, (c + 1) * d)
        u = jnp.maximum(_dot(h2, wu_ref[:, cs]), 0.0)
        acc = acc + _dot((u * u).astype(BF16), wd_ref[cs, :])
    y_ref[...] = _rms(acc, nf_ref[...])


def _out_mlp(x, og, ao, gg, wa, wb, wo, n2, wu, wd, nf, tm):
    rows, d = x.shape
    row = lambda n: pl.BlockSpec((tm, n), lambda i: (i, 0))
    const = lambda a: pl.BlockSpec(a.shape, lambda i: (0, 0), pipeline_mode=pl.Buffered(1))
    return pl.pallas_call(
        _out_kernel,
        grid=(rows // tm,),
        in_specs=[row(d), row(HG_W), row(AT_W), row(2 * d), const(wa), const(wb), const(wo),
                  const(n2), const(wu), const(wd), const(nf)],
        out_specs=row(d),
        out_shape=jax.ShapeDtypeStruct((rows, d), F32),
        compiler_params=_cparams(("arbitrary",)),
        name="out_mlp",
    )(x, og, ao, gg, wa, wb, wo, n2, wu, wd, nf)


def _pack_w_in(w):
    d = w.shape[0]
    pad = lambda n: jnp.zeros((d, n), w.dtype)
    ik0 = _C_IK
    iw0 = ik0 + IDX_DIM
    gg0 = iw0 + IDX_HEADS
    return jnp.concatenate(
        [w[:, :ik0], w[:, ik0:iw0], pad(LANES - IDX_DIM), w[:, iw0:gg0], pad(LANES - IDX_HEADS),
         w[:, gg0:]], axis=1).astype(BF16)


def kernel(x_prompt, x_sample, cache_k, cache_v, cache_idx_k, state_hgrn, page_table, lb_logits,
           w_in, hg_norm, w_a, w_b, w_o, norm1, norm2, w_up, w_down, norm_f):
    depth = w_in.shape[0]
    assert depth == 1, "single-layer stack"
    nb, t, d = x_prompt.shape
    ns, ts, _ = x_sample.shape
    n_phys, page = cache_k.shape[1], cache_k.shape[2]

    w_in_p = _pack_w_in(w_in[0])
    assert w_in_p.shape[1] == _C_END
    bf = lambda a: a.astype(BF16)
    wa, wb, wo, wu, wd = bf(w_a[0]), bf(w_b[0]), bf(w_o[0]), bf(w_up[0]), bf(w_down[0])
    n1, n2, nf = norm1[0][None], norm2[0][None], norm_f[None]
    hgn = hg_norm[0][None]

    xp = x_prompt.reshape(nb * t, d)
    xs = x_sample.reshape(ns * ts, d)
    tm_p, tm_s = min(ROW_TILE, nb * t), min(ROW_TILE, ns * ts)
    zh_p, aq_p, ak_p, av_p, akb_p, avb_p, iq_p, ik_p, ikb_p, iw_p, gg_p = _inproj(xp, n1, w_in_p, tm_p)
    zh_s, aq_s, ak_s, av_s, _, _, iq_s, ik_s, _, iw_s, gg_s = _inproj(xs, n1, w_in_p, tm_s)

    og_p, st_p = _hgrn_prompt(lb_logits, hgn, zh_p.reshape(nb, t, 4 * HG_W))
    og_s, st_s = _hgrn_sample(lb_logits, hgn, zh_s, state_hgrn[0], ts, 8)

    ao_p = _attn_prompt(aq_p, iq_p, iw_p, akb_p, avb_p, ikb_p, nb, min(SELECT_ROWS, t))

    tp = -(-ts // 8) * 8
    tok = lambda a: jnp.pad(a.reshape(ns, ts, -1), ((0, 0), (0, tp - ts), (0, 0)))
    ao_s = _attn_sample(
        page_table,
        tok(aq_s.astype(F32)),
        tok(iq_s).reshape(ns, tp * IDX_HEADS, IDX_DIM),
        tok(iw_s).reshape(ns, tp * IDX_HEADS, 1),
        tok(ak_s), tok(av_s), tok(ik_s),
        jnp.swapaxes(cache_idx_k, 2, 3).reshape(n_phys, IDX_DIM, page),
        cache_k.reshape(n_phys, page * AT_HEADS, AT_DH), cache_v.reshape(n_phys, page * AT_HEADS, AT_DH),
        ts)
    ao_s = ao_s[:, :ts].reshape(ns * ts, AT_W).astype(BF16)

    y_p = _out_mlp(xp, og_p.reshape(nb * t, HG_W), ao_p, gg_p, wa, wb, wo, n2, wu, wd, nf, tm_p)
    y_s = _out_mlp(xs, og_s, ao_s, gg_s, wa, wb, wo, n2, wu, wd, nf, tm_s)

    return (y_p.reshape(nb, t, d), y_s.reshape(ns, ts, d),
            ak_p.reshape(1, nb, t, AT_HEADS, AT_DH), av_p.reshape(1, nb, t, AT_HEADS, AT_DH),
            ik_p.reshape(1, nb, t, IDX_DIM), st_p[None],
            ak_s.reshape(1, ns, ts, AT_HEADS, AT_DH), av_s.reshape(1, ns, ts, AT_HEADS, AT_DH),
            ik_s.reshape(1, ns, ts, IDX_DIM), st_s[None])
```

```python
import functools

import jax
import jax.numpy as jnp
from jax import lax
from jax.experimental import pallas as pl
from jax.experimental.pallas import tpu as pltpu

F32 = jnp.float32
BF16 = jnp.bfloat16
I32 = jnp.int32

EPS = 1e-6
HG_HEADS = 4
HG_D = 128
HG_W = HG_HEADS * HG_D
HG_CHUNK = 64
HG_STEP_CHUNKS = 4
AT_HEADS = 4
AT_DH = 128
AT_W = AT_HEADS * AT_DH
IDX_HEADS = 8
IDX_DIM = 64
IDX_W = IDX_HEADS * IDX_DIM
IDX_SCALE = (IDX_HEADS * IDX_DIM) ** -0.5
TOPK_MAX = 256
LANES = 128
INT_MIN = -(2 ** 31)
NEG_BIG = -1e30
VMEM_LIMIT = 56 * 1024 * 1024
ROW_TILE = 512
KEY_GROUP = 512
SELECT_SEQS = 16
SELECT_ROWS = 256
SUB_ROWS = 128
ATT_UNROLL = 4

_C_ZH = 0
_C_AQ = 2048
_C_AK = 2560
_C_AV = 3072
_C_IQ = 3584
_C_IK = 4096
_C_IW = 4224
_C_GG = 4352
_C_END = 6400


def _dot(a, b, **kw):
    return jnp.dot(a, b, preferred_element_type=F32, **kw)


def _dot_nt(a, b):
    return lax.dot_general(a, b, (((1,), (1,)), ((), ())), preferred_element_type=F32)


def _dot_tn(a, b):
    return lax.dot_general(a, b, (((0,), (0,)), ((), ())), preferred_element_type=F32)


def _rms(x, g):
    return x * lax.rsqrt(jnp.mean(x * x, axis=-1, keepdims=True) + EPS) * g


def _sigmoid(x):
    return 1.0 / (1.0 + jnp.exp(-x))


def _cparams(sem):
    return pltpu.CompilerParams(dimension_semantics=sem, vmem_limit_bytes=VMEM_LIMIT)


def _inproj_kernel(x_ref, g_ref, w_ref, zh_ref, aq_ref, ak_ref, av_ref, akb_ref, avb_ref,
                   iq_ref, ik_ref, ikb_ref, iw_ref, gg_ref):
    h = _rms(x_ref[...], g_ref[...]).astype(BF16)

    def mm(a, n):
        return _dot(h, w_ref[:, a:a + n])

    for c in range(4):
        zh_ref[:, c * 512:(c + 1) * 512] = mm(_C_ZH + c * 512, 512)
    aq_ref[...] = mm(_C_AQ, AT_W).astype(BF16)
    tm = x_ref.shape[0]
    for c0, f_ref, b_ref in ((_C_AK, ak_ref, akb_ref), (_C_AV, av_ref, avb_ref)):
        kv = mm(c0, AT_W)
        b_ref[...] = kv.astype(BF16)
        for hd in range(AT_HEADS):
            f_ref[pl.ds(hd, tm, stride=AT_HEADS), :] = kv[:, hd * AT_DH:(hd + 1) * AT_DH]
    iq_ref[...] = mm(_C_IQ, IDX_W).astype(BF16)
    ik = mm(_C_IK, LANES)[:, :IDX_DIM]
    ik_ref[...] = ik
    ikb_ref[...] = ik.astype(BF16)
    iw_ref[...] = mm(_C_IW, LANES)[:, :IDX_HEADS]
    for c in range(4):
        gg_ref[:, c * 512:(c + 1) * 512] = mm(_C_GG + c * 512, 512)


def _inproj(x, g, w, tm):
    rows, d = x.shape
    row = lambda n, r=1: pl.BlockSpec((tm * r, n), lambda i: (i, 0))
    outs = [(2048, F32, 1), (AT_W, BF16, 1), (AT_DH, F32, AT_HEADS), (AT_DH, F32, AT_HEADS),
            (AT_W, BF16, 1), (AT_W, BF16, 1), (IDX_W, BF16, 1), (IDX_DIM, F32, 1),
            (IDX_DIM, BF16, 1), (IDX_HEADS, F32, 1), (2048, F32, 1)]
    return pl.pallas_call(
        _inproj_kernel,
        grid=(rows // tm,),
        in_specs=[row(d), pl.BlockSpec((1, d), lambda i: (0, 0)),
                  pl.BlockSpec(w.shape, lambda i: (0, 0), pipeline_mode=pl.Buffered(1))],
        out_specs=[row(n, r) for n, _, r in outs],
        out_shape=[jax.ShapeDtypeStruct((rows * r, n), t) for n, t, r in outs],
        compiler_params=_cparams(("arbitrary",)),
        name="inproj",
    )(x, g, w)


def _lower_bound(lbl):
    e = jnp.exp(lbl - jnp.max(lbl, axis=0, keepdims=True))
    return e[0:1] / jnp.sum(e, axis=0, keepdims=True)


def _forget(hf, lb):
    f = lb + (1.0 - lb) * _sigmoid(hf)
    return jnp.log(f), 1.0 - f


def _group_row(x, gsize, r):
    n, w = x.shape
    g = x.reshape(n // gsize, gsize, w)
    return jnp.broadcast_to(g[:, r:r + 1, :], g.shape).reshape(n, w)


def _hg_out(o, hog, hgn):
    return (_rms(o, hgn) * (hog * _sigmoid(hog))).astype(BF16)


def _hgrn_prompt_kernel(lbl_ref, hgn_ref, z_ref, o_ref, st_ref, s_scr):
    i = pl.program_id(0)
    nb = z_ref.shape[0]
    c = HG_CHUNK

    @pl.when(i == 0)
    def _():
        s_scr[...] = jnp.zeros_like(s_scr)

    lb = _lower_bound(lbl_ref[...])
    hgn = hgn_ref[...]
    ti = lax.broadcasted_iota(I32, (c, c), 0)
    si = lax.broadcasted_iota(I32, (c, c), 1)
    tril = (si <= ti).astype(F32)
    rowl = lax.broadcasted_iota(I32, (c, 1), 0)
    same = {g: (ti // g) == (si // g) for g in (32, 16, 8)}
    diag_mask = same[8] & (si <= ti)

    for ci, bi in ((ci, bi) for ci in range(z_ref.shape[1] // c) for bi in range(nb)):
        rows = slice(ci * c, (ci + 1) * c)
        z = z_ref[bi, rows, :]
        hq, hf, hi, hog = (z[:, k * HG_W:(k + 1) * HG_W] for k in range(4))
        logf, kk = _forget(hf, lb)
        b = _dot(tril, logf, precision=lax.Precision.HIGHEST)

        qs, ks = [], []
        for m in (32, 16, 8):
            upper = (rowl % (2 * m)) >= m
            e = jnp.exp(-jnp.abs(b - _group_row(b, 2 * m, m - 1)))
            qs.append(jnp.where(upper, hq * e, 0.0).astype(BF16))
            ks.append(jnp.where(upper, 0.0, kk * e).astype(BF16))
        mid = 0.5 * (_group_row(b - logf, 8, 0) + _group_row(b, 8, 7))
        qs.append((hq * jnp.exp(b - mid)).astype(BF16))
        ks.append((kk * jnp.exp(mid - b)).astype(BF16))

        qe = (hq * jnp.exp(b)).astype(BF16)
        bend = b[c - 1:c, :]
        kdec = (kk * jnp.exp(bend - b)).astype(BF16)
        ebend = jnp.exp(bend)
        vb = hi.astype(BF16)

        for h in range(HG_HEADS):
            sl = slice(h * HG_D, (h + 1) * HG_D)
            p32, p16, p8, pd = (_dot_nt(q[:, sl], k[:, sl]) for q, k in zip(qs, ks))
            a = (p32 + jnp.where(same[32], p16, 0.0) + jnp.where(same[16], p8, 0.0)
                 + jnp.where(diag_mask, pd, 0.0))
            st = s_scr[bi, h]
            o = _dot_nt(qe[:, sl], st.astype(BF16)) + _dot(a.astype(BF16), vb[:, sl])
            st_new = st * ebend[:, sl] + _dot_tn(vb[:, sl], kdec[:, sl])
            s_scr[bi, h] = st_new
            o_ref[bi, rows, sl] = _hg_out(o, hog[:, sl], hgn)

    @pl.when(i == pl.num_programs(0) - 1)
    def _():
        for bi in range(nb):
            for h in range(HG_HEADS):
                st_ref[bi, h] = s_scr[bi, h].T


def _hgrn_prompt(lbl, hgn, zh):
    nb, t, _ = zh.shape
    c = min(HG_CHUNK * HG_STEP_CHUNKS, t)
    assert t % c == 0 and c % HG_CHUNK == 0
    return pl.pallas_call(
        _hgrn_prompt_kernel,
        grid=(t // c,),
        in_specs=[pl.BlockSpec(lbl.shape, lambda i: (0, 0)),
                  pl.BlockSpec(hgn.shape, lambda i: (0, 0)),
                  pl.BlockSpec((nb, c, 4 * HG_W), lambda i: (0, i, 0))],
        out_specs=[pl.BlockSpec((nb, c, HG_W), lambda i: (0, i, 0)),
                   pl.BlockSpec((nb, HG_HEADS, HG_D, HG_D), lambda i: (0, 0, 0, 0))],
        out_shape=[jax.ShapeDtypeStruct((nb, t, HG_W), BF16),
                   jax.ShapeDtypeStruct((nb, HG_HEADS, HG_D, HG_D), F32)],
        scratch_shapes=[pltpu.VMEM((nb, HG_HEADS, HG_D, HG_D), F32)],
        compiler_params=_cparams(("arbitrary",)),
        name="hgrn_prompt",
    )(lbl, hgn, zh)


def _hgrn_sample_kernel(lbl_ref, hgn_ref, z_ref, s0_ref, o_ref, st_ref, *, t):
    rows = z_ref.shape[0]
    nseq = rows // t
    lb = _lower_bound(lbl_ref[...])
    hgn = hgn_ref[...]
    z = z_ref[...]
    hq, hf, hi, hog = (z[:, k * HG_W:(k + 1) * HG_W] for k in range(4))
    logf, kk = _forget(hf, lb)
    tl = lax.broadcasted_iota(I32, (rows, 1), 0) % t

    li = lax.broadcasted_iota(I32, (HG_W, HG_W), 0) // HG_D
    lj = lax.broadcasted_iota(I32, (HG_W, HG_W), 1) // HG_D
    head_ones = (li == lj).astype(BF16)

    shift = lambda x, n: pltpu.roll(x, n % rows, axis=0)
    dsum = jnp.zeros_like(logf)
    b = jnp.zeros_like(logf)
    suf = jnp.zeros_like(logf)
    o_intra = jnp.zeros_like(logf)
    for delta in range(t):
        ok = tl >= delta
        if delta > 0:
            dsum = dsum + shift(logf, delta - 1)
            suf = suf + jnp.where(tl + delta < t, shift(logf, -delta), 0.0)
        b = b + jnp.where(ok, shift(logf, delta), 0.0)
        term = jnp.where(ok, hq * shift(kk, delta) * jnp.exp(dsum), 0.0)
        a = _dot(term.astype(BF16), head_ones)
        o_intra = o_intra + a * shift(hi, delta)

    qe = (hq * jnp.exp(b)).astype(BF16)
    kdec = kk * jnp.exp(suf)
    vb = hi.astype(BF16)
    eb = jnp.exp(b + suf)

    seq_of_row = lax.broadcasted_iota(I32, (rows, 1), 0) // t
    for h in range(HG_HEADS):
        sl = slice(h * HG_D, (h + 1) * HG_D)
        eb_t = eb[:, sl].T
        o = o_intra[:, sl]
        for s in range(nseq):
            mine = seq_of_row == s
            s0 = s0_ref[s, h]
            o = o + jnp.where(mine, _dot(qe[:, sl], s0.astype(BF16)), 0.0)
            kd = jnp.where(mine, kdec[:, sl], 0.0).astype(BF16)
            st_ref[s, h] = s0 * eb_t[:, s * t:s * t + 1] + _dot_tn(kd, vb[:, sl])
        o_ref[:, sl] = _hg_out(o, hog[:, sl], hgn)


def _hgrn_sample(lbl, hgn, zh, s0, t, seq_blk):
    rows = zh.shape[0]
    nseq = rows // t
    rb = seq_blk * t
    return pl.pallas_call(
        functools.partial(_hgrn_sample_kernel, t=t),
        grid=(nseq // seq_blk,),
        in_specs=[pl.BlockSpec(lbl.shape, lambda i: (0, 0)),
                  pl.BlockSpec(hgn.shape, lambda i: (0, 0)),
                  pl.BlockSpec((rb, 4 * HG_W), lambda i: (i, 0)),
                  pl.BlockSpec((seq_blk, HG_HEADS, HG_D, HG_D), lambda i: (i, 0, 0, 0))],
        out_specs=[pl.BlockSpec((rb, HG_W), lambda i: (i, 0)),
                   pl.BlockSpec((seq_blk, HG_HEADS, HG_D, HG_D), lambda i: (i, 0, 0, 0))],
        out_shape=[jax.ShapeDtypeStruct((rows, HG_W), BF16),
                   jax.ShapeDtypeStruct(s0.shape, F32)],
        compiler_params=_cparams(("arbitrary",)),
        name="hgrn_sample",
    )(lbl, hgn, zh, s0)


def _key_to_float(key):
    bits = key ^ ((key >> 31) & 0x7FFFFFFF)
    return lax.bitcast_convert_type(bits, F32)


def _bit_search(accept, nbits, start):
    def body(k, x):
        cand = x | lax.shift_left(jnp.int32(1), jnp.asarray(nbits - 1 - k, I32))
        return jnp.where(accept(cand), cand, x)
    return lax.fori_loop(0, nbits, body, start)


def _any(x):
    return jnp.max(jnp.where(x, 1.0, 0.0)) > 0.5


def _float_to_key(x):
    bits = lax.bitcast_convert_type(x, I32)
    return bits ^ ((bits >> 31) & 0x7FFFFFFF)


def _grid_key_to_float(key):
    return lax.bitcast_convert_type(lax.shift_left(key ^ ((key >> 31) & 0x7FFF), 16), F32)


def _floor_to_grid(x):
    bits = lax.bitcast_convert_type(x, I32)
    down = (bits + ((bits >> 31) & 0xFFFF)) & jnp.int32(-65536)
    return lax.bitcast_convert_type(down, F32).astype(BF16)


def _select(need, count, masked_min, nbits_idx, count_floor_ge=None):
    shape = need.shape
    ge = lambda x: count(lambda s, kp, x: s >= x, x)
    zero = jnp.zeros(shape, I32)

    def search_all_bits():
        start = jnp.where(ge(jnp.zeros(shape, F32)) >= need, 0, INT_MIN).astype(I32)
        tkey = _bit_search(lambda k: ge(_key_to_float(k)) >= need, 31, start)
        return tkey, ge(_key_to_float(tkey))

    if count_floor_ge is None:
        tkey, n_ge = search_all_bits()
    else:
        start = jnp.where(count_floor_ge(jnp.zeros(shape, F32)) >= need, 0, -(2 ** 15)).astype(I32)
        coarse = _bit_search(lambda k: count_floor_ge(_grid_key_to_float(k)) >= need, 15, start)
        base = _float_to_key(_grid_key_to_float(coarse))
        tkey = base + _bit_search(lambda o: ge(_key_to_float(base + o)) >= need, 16, zero)
        n_ge = ge(_key_to_float(tkey))
        tkey, n_ge = lax.cond(_any(n_ge < need), search_all_bits, lambda: (tkey, n_ge))
    tf = _key_to_float(tkey)

    in_set = lambda s, kp, tf, lo: (s >= tf) & (s > lo)

    def stats(lo):
        m = masked_min(in_set, tf, lo)
        return m, count(lambda s, kp, m: s == m, m)

    droppable = lambda extra, cm: (extra > 0) & (extra >= cm)

    def drop(st):
        lo, extra, m, cm = st
        d = droppable(extra, cm)
        lo = jnp.where(d, m, lo)
        return (lo, jnp.where(d, extra - cm, extra)) + stats(lo)

    lo = jnp.full(shape, -jnp.inf, F32)
    extra = n_ge - need
    _, extra, m, cm = lax.while_loop(lambda st: _any(droppable(st[1], st[3])), drop,
                                     (lo, extra) + stats(lo))
    keep = cm - extra
    ties_before = lambda x: count(lambda s, kp, m, x: (s == m) & (kp < x), m, x)
    cut = lax.cond(
        _any(extra > 0),
        lambda: _bit_search(lambda x: ties_before(x) < keep, nbits_idx, jnp.zeros(shape, I32)),
        lambda: jnp.full(shape, 2 ** nbits_idx - 1, I32))
    return m, cut


def _attn_prompt_kernel(aq_ref, iq_ref, iw_ref, kb_ref, vb_ref, ikb_ref, o_ref,
                        sc_scr, fl_scr, w_scr, iq_scr, lg_scr, *, n_sel):
    qb = aq_ref.shape[0]
    kg = sc_scr.shape[2]
    nc = kg // LANES
    i = pl.program_id(1)
    ng = ((i + 1) * qb + kg - 1) // kg
    qpos = i * qb + lax.broadcasted_iota(I32, (qb, 1), 0)
    lane = lax.broadcasted_iota(I32, (1, LANES), 1)
    kpos_of = lambda g, c: g * kg + c * LANES + lane
    rows_of = lambda g: pl.ds(pl.multiple_of(g * kg, kg), kg)

    iw = iw_ref[...]
    for h in range(IDX_HEADS):
        w_scr[h] = jnp.broadcast_to(iw[:, h:h + 1], (qb, LANES))
        iq_scr[h] = iq_ref[:, h * IDX_DIM:(h + 1) * IDX_DIM]

    def sweep(body, init, n):
        def trip(j, c):
            for k in range(ATT_UNROLL):
                c = body(j * ATT_UNROLL + k, c)
            return c
        c = lax.fori_loop(0, n // ATT_UNROLL, trip, init)
        for k in range(ATT_UNROLL - 1, 0, -1):
            c = lax.cond(n % ATT_UNROLL >= k, lambda c, k=k: body(n - k, c), lambda c: c, c)
        return c

    nsub = qb // SUB_ROWS
    sub_rows = [slice(u * SUB_ROWS, (u + 1) * SUB_ROWS) for u in range(nsub)]
    sub_ng = [(i * qb + (u + 1) * SUB_ROWS + kg - 1) // kg for u in range(nsub)]

    for rs, ng_u in zip(sub_rows, sub_ng):
        def score_group(g, carry, rs=rs):
            ikt = ikb_ref[rows_of(g), :]
            acc = [jnp.zeros((SUB_ROWS, LANES), F32) for _ in range(nc)]
            for h in range(IDX_HEADS):
                d = jnp.maximum(_dot_nt(iq_scr[h, rs, :], ikt), 0.0)
                w = w_scr[h, rs, :]
                for c in range(nc):
                    acc[c] = acc[c] + d[:, c * LANES:(c + 1) * LANES] * w
            for c in range(nc):
                cs = slice(c * LANES, (c + 1) * LANES)
                sc = jnp.where(kpos_of(g, c) <= qpos[rs], acc[c] * IDX_SCALE, -jnp.inf)
                sc_scr[g, rs, cs] = sc
                fl_scr[g, rs, cs] = _floor_to_grid(sc)
            return carry

        def no_keys(g, carry, rs=rs):
            sc_scr[g, rs, :] = jnp.full((SUB_ROWS, kg), -jnp.inf, F32)
            fl_scr[g, rs, :] = jnp.full((SUB_ROWS, kg), -jnp.inf, BF16)
            return carry

        sweep(score_group, 0, ng_u)
        lax.fori_loop(ng_u, ng, no_keys, 0)

    def reduce_keys(pred, ops, init, elem, combine, lane_reduce):
        tiled = [jnp.broadcast_to(x, (qb, LANES)) for x in ops]
        accs = []
        for rs in sub_rows:
            def body(g, acc, rs=rs, mine=[x[rs] for x in tiled]):
                for c in range(nc):
                    s = sc_scr[g, rs, c * LANES:(c + 1) * LANES]
                    acc = combine(acc, elem(pred(s, kpos_of(g, c), *mine), s))
                return acc
            accs.append(lax.fori_loop(0, ng, body, jnp.full((SUB_ROWS, LANES), init, F32)))
        return lane_reduce(jnp.concatenate(accs, axis=0), axis=1, keepdims=True)

    count = lambda pred, *ops: reduce_keys(pred, ops, 0.0, lambda p, s: jnp.where(p, 1.0, 0.0),
                                           lambda a, b: a + b, jnp.sum)
    masked_min = lambda pred, *ops: reduce_keys(
        pred, ops, jnp.inf, lambda p, s: jnp.where(p, s, jnp.inf), jnp.minimum, jnp.min)
    need = jnp.minimum(qpos + 1, n_sel).astype(F32)
    def count_floor_ge(x):
        xb = jnp.broadcast_to(x, (qb, LANES)).astype(BF16)
        one, zero = jnp.ones((), BF16), jnp.zeros((), BF16)
        accs = []
        for rs in sub_rows:
            def body(g, acc, rs=rs, mine=xb[rs]):
                for c in range(nc):
                    acc = acc + jnp.where(fl_scr[g, rs, c * LANES:(c + 1) * LANES] >= mine, one, zero)
                return acc
            accs.append(lax.fori_loop(0, ng, body, jnp.zeros((SUB_ROWS, LANES), BF16)))
        return jnp.sum(jnp.concatenate(accs, axis=0).astype(F32), axis=1, keepdims=True)

    assert (kb_ref.shape[0] // kg) * nc <= 256
    m, cut = _select(need, count, masked_min, max(1, (kb_ref.shape[0] - 1).bit_length()),
                     count_floor_ge)

    def bias_group(g, carry):
        for c in range(nc):
            cs = slice(c * LANES, (c + 1) * LANES)
            s = sc_scr[g, :, cs]
            sel = (s > m) | ((s == m) & (kpos_of(g, c) <= cut))
            sc_scr[g, :, cs] = jnp.where(sel, 0.0, NEG_BIG)
        return carry

    lax.fori_loop(0, ng, bias_group, 0)

    scale = AT_DH ** -0.5
    heads = [slice(h * AT_DH, (h + 1) * AT_DH) for h in range(AT_HEADS)]

    for rs, ng_u in zip(sub_rows, sub_ng):
        def logits_group(g, mrun, rs=rs):
            bias = sc_scr[g, rs, :]
            out = []
            for h, sl in enumerate(heads):
                t = _dot_nt(aq_ref[rs, sl], kb_ref[rows_of(g), sl]) * scale + bias
                lg_scr[h, g] = t
                mt = mrun[h]
                for c in range(nc):
                    mt = jnp.maximum(mt, t[:, c * LANES:(c + 1) * LANES])
                out.append(mt)
            return tuple(out)

        mrun = sweep(logits_group,
                     tuple(jnp.full((SUB_ROWS, LANES), NEG_BIG, F32) for _ in heads), ng_u)
        for h in range(AT_HEADS):
            w_scr[h, rs, :] = jnp.broadcast_to(jnp.max(mrun[h], axis=1, keepdims=True),
                                               (SUB_ROWS, LANES))

        def pv_group(g, carry, rs=rs):
            out = []
            for h, (sl, (lsum, acc)) in enumerate(zip(heads, carry)):
                mx = w_scr[h, rs, :]
                ps = [jnp.exp(lg_scr[h, g, :, c * LANES:(c + 1) * LANES] - mx)
                      for c in range(nc)]
                for p in ps:
                    lsum = lsum + p
                pb = jnp.concatenate([p.astype(BF16) for p in ps], axis=1)
                out.append((lsum, acc + _dot(pb, vb_ref[rows_of(g), sl])))
            return tuple(out)

        zeros = jnp.zeros((SUB_ROWS, LANES), F32)
        stats = sweep(pv_group, tuple((zeros, jnp.zeros((SUB_ROWS, AT_DH), F32)) for _ in heads),
                      ng_u)
        for sl, (lsum, acc) in zip(heads, stats):
            o_ref[rs, sl] = (acc / jnp.sum(lsum, axis=1, keepdims=True)).astype(BF16)


def _attn_prompt(aq, iq, iw, kb, vb, ikb, nb, qb):
    rows = aq.shape[0]
    t = rows // nb
    n_sel = min(TOPK_MAX, t // 4)
    nq = t // qb
    kg = min(KEY_GROUP, t)
    assert t % kg == 0 and kg % LANES == 0 and t % qb == 0 and qb % SUB_ROWS == 0
    qrow = lambda n: pl.BlockSpec((qb, n), lambda b, i: (b * nq + i, 0))
    seq = lambda n: pl.BlockSpec((t, n), lambda b, i: (b, 0), pipeline_mode=pl.Buffered(1))
    return pl.pallas_call(
        functools.partial(_attn_prompt_kernel, n_sel=n_sel),
        grid=(nb, nq),
        in_specs=[qrow(AT_W), qrow(IDX_W), qrow(IDX_HEADS), seq(AT_W), seq(AT_W), seq(IDX_DIM)],
        out_specs=qrow(AT_W),
        out_shape=jax.ShapeDtypeStruct((rows, AT_W), BF16),
        scratch_shapes=[pltpu.VMEM((t // kg, qb, kg), F32),
                        pltpu.VMEM((t // kg, qb, kg), BF16),
                        pltpu.VMEM((IDX_HEADS, qb, LANES), F32),
                        pltpu.VMEM((IDX_HEADS, qb, IDX_DIM), BF16),
                        pltpu.VMEM((AT_HEADS, t // kg, SUB_ROWS, kg), F32)],
        compiler_params=_cparams(("arbitrary", "arbitrary")),
        name="attn_prompt",
    )(aq, iq, iw, kb, vb, ikb)


def _pad_rows(x, rows):
    return jnp.concatenate([x, jnp.zeros((rows - x.shape[0], x.shape[1]), x.dtype)], axis=0)


def _select_sample_kernel(pt_ref, iq_ref, iw_ref, ikn_ref, *rest, n_pages, n_sel, t, seq_blk):
    del pt_ref
    ik_pages = rest[:n_pages]
    bias_ref, sc_scr = rest[n_pages:]
    page = ik_pages[0].shape[1]
    past = n_pages * page
    total = past + page
    nc = total // LANES
    slot = pl.program_id(0) % seq_blk
    rows = pl.ds(pl.multiple_of(slot * t, t), t)
    lane = lax.broadcasted_iota(I32, (1, LANES), 1)

    iq = iq_ref[...]
    w = iw_ref[...]

    def head_sum(d):
        r = jnp.maximum(d, 0.0) * w
        return jnp.sum(r.reshape(t, IDX_HEADS, d.shape[1]), axis=1) * IDX_SCALE

    for p in range(n_pages):
        sc_scr[rows, p * page:(p + 1) * page] = head_sum(_dot(iq, ik_pages[p][...].astype(BF16)))
    new = head_sum(_dot_nt(iq, _pad_rows(ikn_ref[...], page).astype(BF16)))
    tok_pos = lax.broadcasted_iota(I32, (t, 1), 0)
    sc_scr[rows, past:total] = jnp.where(lane <= tok_pos, new, -jnp.inf)

    @pl.when(slot == seq_blk - 1)
    def _():
        nrow = seq_blk * t
        qpos = past + lax.broadcasted_iota(I32, (nrow, 1), 0) % t
        kpos_of = lambda c: c * LANES + lane

        def reduce_keys(pred, ops, init, elem, combine, lane_reduce):
            tiled = [jnp.broadcast_to(x, (nrow, LANES)) for x in ops]
            acc = jnp.full((nrow, LANES), init, F32)
            for c in range(nc):
                s = sc_scr[:, c * LANES:(c + 1) * LANES]
                acc = combine(acc, elem(pred(s, kpos_of(c), *tiled), s))
            return lane_reduce(acc, axis=1, keepdims=True)

        count = lambda pred, *ops: reduce_keys(
            pred, ops, 0.0, lambda p, s: jnp.where(p, 1.0, 0.0), lambda a, b: a + b, jnp.sum)
        masked_min = lambda pred, *ops: reduce_keys(
            pred, ops, jnp.inf, lambda p, s: jnp.where(p, s, jnp.inf), jnp.minimum, jnp.min)
        need = jnp.minimum(qpos + 1, n_sel).astype(F32)
        m, cut = _select(need, count, masked_min, max(1, (total - 1).bit_length()))
        for c in range(nc):
            s = sc_scr[:, c * LANES:(c + 1) * LANES]
            sel = (s > m) | ((s == m) & (kpos_of(c) <= cut))
            bias_ref[:, :, c * LANES:(c + 1) * LANES] = jnp.where(sel, 0.0, NEG_BIG).reshape(
                seq_blk, t, LANES)


def _attend_sample_kernel(pt_ref, aq_ref, bias_ref, kn_ref, vn_ref, *rest, n_pages, t):
    del pt_ref
    k_pages = rest[:n_pages]
    v_pages = rest[n_pages:2 * n_pages]
    o_ref, kb_scr, vb_scr = rest[2 * n_pages:]
    page = k_pages[0].shape[0] // AT_HEADS
    past = n_pages * page
    total = past + page

    for p in range(n_pages):
        rows = slice(p * page, (p + 1) * page)
        for h in range(AT_HEADS):
            head_rows = pl.ds(h, page, stride=AT_HEADS)
            kb_scr[h, rows, :] = k_pages[p][head_rows, :].astype(BF16)
            vb_scr[h, rows, :] = v_pages[p][head_rows, :].astype(BF16)
    tail = slice(past, total)
    for h in range(AT_HEADS):
        sl = slice(h * AT_DH, (h + 1) * AT_DH)
        kb_scr[h, tail, :] = _pad_rows(kn_ref[:, sl], page).astype(BF16)
        vb_scr[h, tail, :] = _pad_rows(vn_ref[:, sl], page).astype(BF16)

    scale = AT_DH ** -0.5
    aq = aq_ref[...].astype(BF16)
    bias = bias_ref[...]
    for h in range(AT_HEADS):
        sl = slice(h * AT_DH, (h + 1) * AT_DH)
        s = _dot_nt(aq[:, sl], kb_scr[h]) * scale + bias
        p = jnp.exp(s - jnp.max(s, axis=1, keepdims=True))
        l = jnp.sum(p, axis=1, keepdims=True)
        o_ref[:, sl] = _dot(p.astype(BF16), vb_scr[h]) / l


def _attn_sample(page_table, aq, iq, iw, kn, vn, ikn, cache_ikt, cache_k, cache_v, t_real):
    nseq, n_pages = page_table.shape
    page = cache_ikt.shape[2]
    past = n_pages * page
    total = past + page
    t = aq.shape[1]
    n_sel = min(TOPK_MAX, (past + t_real) // 4)
    seq_blk = min(SELECT_SEQS, nseq)
    assert nseq % seq_blk == 0
    tok = lambda r, n: pl.BlockSpec((None, r, n), lambda s, pt: (s, 0, 0))
    pg = lambda r, n, p: pl.BlockSpec((None, r, n), lambda s, pt, p=p: (pt[s, p], 0, 0))

    bias = pl.pallas_call(
        functools.partial(_select_sample_kernel, n_pages=n_pages, n_sel=n_sel, t=t,
                          seq_blk=seq_blk),
        grid_spec=pltpu.PrefetchScalarGridSpec(
            num_scalar_prefetch=1,
            grid=(nseq,),
            in_specs=[tok(t * IDX_HEADS, IDX_DIM), tok(t * IDX_HEADS, 1), tok(t, IDX_DIM)]
            + [pg(IDX_DIM, page, p) for p in range(n_pages)],
            out_specs=pl.BlockSpec((seq_blk, t, total), lambda s, pt: (s // seq_blk, 0, 0)),
            scratch_shapes=[pltpu.VMEM((seq_blk * t, total), F32)]),
        out_shape=jax.ShapeDtypeStruct((nseq, t, total), F32),
        compiler_params=_cparams(("arbitrary",)),
        name="select_sample",
    )(page_table, iq, iw, ikn, *([cache_ikt] * n_pages))

    return pl.pallas_call(
        functools.partial(_attend_sample_kernel, n_pages=n_pages, t=t),
        grid_spec=pltpu.PrefetchScalarGridSpec(
            num_scalar_prefetch=1,
            grid=(nseq,),
            in_specs=[tok(t, AT_W), tok(t, total), tok(t, AT_W), tok(t, AT_W)]
            + [pg(page * AT_HEADS, AT_DH, p) for p in range(n_pages)] * 2,
            out_specs=tok(t, AT_W),
            scratch_shapes=[pltpu.VMEM((AT_HEADS, total, AT_DH), BF16),
                            pltpu.VMEM((AT_HEADS, total, AT_DH), BF16)]),
        out_shape=jax.ShapeDtypeStruct((nseq, t, AT_W), F32),
        compiler_params=_cparams(("arbitrary",)),
        name="attend_sample",
    )(page_table, aq, bias, kn, vn, *([cache_k] * n_pages), *([cache_v] * n_pages))


def _out_kernel(x_ref, og_ref, ao_ref, gg_ref, wa_ref, wb_ref, wo_ref, n2_ref, wu_ref, wd_ref,
                nf_ref, y_ref):
    d = x_ref.shape[1]
    ya = _dot(og_ref[...], wa_ref[...])
    yb = _dot(ao_ref[...], wb_ref[...])
    m = _sigmoid(gg_ref[:, :d]) * ya + _sigmoid(gg_ref[:, d:]) * yb
    x1 = x_ref[...] + _dot(m.astype(BF16), wo_ref[...])
    h2 = _rms(x1, n2_ref[...]).astype(BF16)
    acc = x1
    ff = wu_ref.shape[1]
    for c in range(ff // d):
        cs = slice(c * d, (c + 1) * d)
        u = jnp.maximum(_dot(h2, wu_ref[:, cs]), 0.0)
        acc = acc + _dot((u * u).astype(BF16), wd_ref[cs, :])
    y_ref[...] = _rms(acc, nf_ref[...])


def _out_mlp(x, og, ao, gg, wa, wb, wo, n2, wu, wd, nf, tm):
    rows, d = x.shape
    row = lambda n: pl.BlockSpec((tm, n), lambda i: (i, 0))
    const = lambda a: pl.BlockSpec(a.shape, lambda i: (0, 0), pipeline_mode=pl.Buffered(1))
    return pl.pallas_call(
        _out_kernel,
        grid=(rows // tm,),
        in_specs=[row(d), row(HG_W), row(AT_W), row(2 * d), const(wa), const(wb), const(wo),
                  const(n2), const(wu), const(wd), const(nf)],
        out_specs=row(d),
        out_shape=jax.ShapeDtypeStruct((rows, d), F32),
        compiler_params=_cparams(("arbitrary",)),
        name="out_mlp",
    )(x, og, ao, gg, wa, wb, wo, n2, wu, wd, nf)


def _pack_w_in(w):
    d = w.shape[0]
    pad = lambda n: jnp.zeros((d, n), w.dtype)
    ik0 = _C_IK
    iw0 = ik0 + IDX_DIM
    gg0 = iw0 + IDX_HEADS
    return jnp.concatenate(
        [w[:, :ik0], w[:, ik0:iw0], pad(LANES - IDX_DIM), w[:, iw0:gg0], pad(LANES - IDX_HEADS),
         w[:, gg0:]], axis=1).astype(BF16)


def kernel(x_prompt, x_sample, cache_k, cache_v, cache_idx_k, state_hgrn, page_table, lb_logits,
           w_in, hg_norm, w_a, w_b, w_o, norm1, norm2, w_up, w_down, norm_f):
    depth = w_in.shape[0]
    assert depth == 1, "single-layer stack"
    nb, t, d = x_prompt.shape
    ns, ts, _ = x_sample.shape
    n_phys, page = cache_k.shape[1], cache_k.shape[2]

    w_in_p = _pack_w_in(w_in[0])
    assert w_in_p.shape[1] == _C_END
    bf = lambda a: a.astype(BF16)
    wa, wb, wo, wu, wd = bf(w_a[0]), bf(w_b[0]), bf(w_o[0]), bf(w_up[0]), bf(w_down[0])
    n1, n2, nf = norm1[0][None], norm2[0][None], norm_f[None]
    hgn = hg_norm[0][None]

    xp = x_prompt.reshape(nb * t, d)
    xs = x_sample.reshape(ns * ts, d)
    tm_p, tm_s = min(ROW_TILE, nb * t), min(ROW_TILE, ns * ts)
    zh_p, aq_p, ak_p, av_p, akb_p, avb_p, iq_p, ik_p, ikb_p, iw_p, gg_p = _inproj(xp, n1, w_in_p, tm_p)
    zh_s, aq_s, ak_s, av_s, _, _, iq_s, ik_s, _, iw_s, gg_s = _inproj(xs, n1, w_in_p, tm_s)

    og_p, st_p = _hgrn_prompt(lb_logits, hgn, zh_p.reshape(nb, t, 4 * HG_W))
    og_s, st_s = _hgrn_sample(lb_logits, hgn, zh_s, state_hgrn[0], ts, 8)

    ao_p = _attn_prompt(aq_p, iq_p, iw_p, akb_p, avb_p, ikb_p, nb, min(SELECT_ROWS, t))

    tp = -(-ts // 8) * 8
    tok = lambda a: jnp.pad(a.reshape(ns, ts, -1), ((0, 0), (0, tp - ts), (0, 0)))
    ao_s = _attn_sample(
        page_table,
        tok(aq_s.astype(F32)),
        tok(iq_s).reshape(ns, tp * IDX_HEADS, IDX_DIM),
        tok(iw_s).reshape(ns, tp * IDX_HEADS, 1),
        tok(ak_s), tok(av_s), tok(ik_s),
        jnp.swapaxes(cache_idx_k, 2, 3).reshape(n_phys, IDX_DIM, page),
        cache_k.reshape(n_phys, page * AT_HEADS, AT_DH), cache_v.reshape(n_phys, page * AT_HEADS, AT_DH),
        ts)
    ao_s = ao_s[:, :ts].reshape(ns * ts, AT_W).astype(BF16)

    y_p = _out_mlp(xp, og_p.reshape(nb * t, HG_W), ao_p, gg_p, wa, wb, wo, n2, wu, wd, nf, tm_p)
    y_s = _out_mlp(xs, og_s, ao_s, gg_s, wa, wb, wo, n2, wu, wd, nf, tm_s)

    return (y_p.reshape(nb, t, d), y_s.reshape(ns, ts, d),
            ak_p.reshape(1, nb, t, AT_HEADS, AT_DH), av_p.reshape(1, nb, t, AT_HEADS, AT_DH),
            ik_p.reshape(1, nb, t, IDX_DIM), st_p[None],
            ak_s.reshape(1, ns, ts, AT_HEADS, AT_DH), av_s.reshape(1, ns, ts, AT_HEADS, AT_DH),
            ik_s.reshape(1, ns, ts, IDX_DIM), st_s[None])
```

```python
import functools

import jax
import jax.numpy as jnp
from jax import lax
from jax.experimental import pallas as pl
from jax.experimental.pallas import tpu as pltpu

F32 = jnp.float32
BF16 = jnp.bfloat16
I32 = jnp.int32

EPS = 1e-6
HG_HEADS = 4
HG_D = 128
HG_W = HG_HEADS * HG_D
HG_CHUNK = 64
HG_STEP_CHUNKS = 4
AT_HEADS = 4
AT_DH = 128
AT_W = AT_HEADS * AT_DH
IDX_HEADS = 8
IDX_DIM = 64
IDX_W = IDX_HEADS * IDX_DIM
IDX_SCALE = (IDX_HEADS * IDX_DIM) ** -0.5
TOPK_MAX = 256
LANES = 128
INT_MIN = -(2 ** 31)
NEG_BIG = -1e30
VMEM_LIMIT = 56 * 1024 * 1024
ROW_TILE = 512
KEY_GROUP = 512
SEARCH_BITS = 23
MAX_DROPS = 2
SELECT_SEQS = 16
SELECT_ROWS = 256
SUB_ROWS = 128
ATT_UNROLL = 4

_C_ZH = 0
_C_AQ = 2048
_C_AK = 2560
_C_AV = 3072
_C_IQ = 3584
_C_IK = 4096
_C_IW = 4224
_C_GG = 4352
_C_END = 6400


def _dot(a, b, **kw):
    return jnp.dot(a, b, preferred_element_type=F32, **kw)


def _dot_nt(a, b):
    return lax.dot_general(a, b, (((1,), (1,)), ((), ())), preferred_element_type=F32)


def _dot_tn(a, b):
    return lax.dot_general(a, b, (((0,), (0,)), ((), ())), preferred_element_type=F32)


def _rms(x, g):
    return x * lax.rsqrt(jnp.mean(x * x, axis=-1, keepdims=True) + EPS) * g


def _sigmoid(x):
    return 1.0 / (1.0 + jnp.exp(-x))


def _cparams(sem):
    return pltpu.CompilerParams(dimension_semantics=sem, vmem_limit_bytes=VMEM_LIMIT)


def _inproj_kernel(x_ref, g_ref, w_ref, zh_ref, aq_ref, ak_ref, av_ref, akb_ref, avb_ref,
                   iq_ref, ik_ref, ikb_ref, iw_ref, gg_ref):
    h = _rms(x_ref[...], g_ref[...]).astype(BF16)

    def mm(a, n):
        return _dot(h, w_ref[:, a:a + n])

    for c in range(4):
        zh_ref[:, c * 512:(c + 1) * 512] = mm(_C_ZH + c * 512, 512)
    aq_ref[...] = mm(_C_AQ, AT_W).astype(BF16)
    tm = x_ref.shape[0]
    for c0, f_ref, b_ref in ((_C_AK, ak_ref, akb_ref), (_C_AV, av_ref, avb_ref)):
        kv = mm(c0, AT_W)
        b_ref[...] = kv.astype(BF16)
        for hd in range(AT_HEADS):
            f_ref[pl.ds(hd, tm, stride=AT_HEADS), :] = kv[:, hd * AT_DH:(hd + 1) * AT_DH]
    iq_ref[...] = mm(_C_IQ, IDX_W).astype(BF16)
    ik = mm(_C_IK, LANES)[:, :IDX_DIM]
    ik_ref[...] = ik
    ikb_ref[...] = ik.astype(BF16)
    iw_ref[...] = mm(_C_IW, LANES)[:, :IDX_HEADS]
    for c in range(4):
        gg_ref[:, c * 512:(c + 1) * 512] = mm(_C_GG + c * 512, 512)


def _inproj(x, g, w, tm):
    rows, d = x.shape
    row = lambda n, r=1: pl.BlockSpec((tm * r, n), lambda i: (i, 0))
    outs = [(2048, F32, 1), (AT_W, BF16, 1), (AT_DH, F32, AT_HEADS), (AT_DH, F32, AT_HEADS),
            (AT_W, BF16, 1), (AT_W, BF16, 1), (IDX_W, BF16, 1), (IDX_DIM, F32, 1),
            (IDX_DIM, BF16, 1), (IDX_HEADS, F32, 1), (2048, F32, 1)]
    return pl.pallas_call(
        _inproj_kernel,
        grid=(rows // tm,),
        in_specs=[row(d), pl.BlockSpec((1, d), lambda i: (0, 0)),
                  pl.BlockSpec(w.shape, lambda i: (0, 0), pipeline_mode=pl.Buffered(1))],
        out_specs=[row(n, r) for n, _, r in outs],
        out_shape=[jax.ShapeDtypeStruct((rows * r, n), t) for n, t, r in outs],
        compiler_params=_cparams(("arbitrary",)),
        name="inproj",
    )(x, g, w)


def _lower_bound(lbl):
    e = jnp.exp(lbl - jnp.max(lbl, axis=0, keepdims=True))
    return e[0:1] / jnp.sum(e, axis=0, keepdims=True)


def _forget(hf, lb):
    f = lb + (1.0 - lb) * _sigmoid(hf)
    return jnp.log(f), 1.0 - f


def _group_row(x, gsize, r):
    n, w = x.shape
    g = x.reshape(n // gsize, gsize, w)
    return jnp.broadcast_to(g[:, r:r + 1, :], g.shape).reshape(n, w)


def _hg_out(o, hog, hgn):
    return (_rms(o, hgn) * (hog * _sigmoid(hog))).astype(BF16)


def _hgrn_prompt_kernel(lbl_ref, hgn_ref, z_ref, o_ref, st_ref, s_scr):
    i = pl.program_id(0)
    nb = z_ref.shape[0]
    c = HG_CHUNK

    @pl.when(i == 0)
    def _():
        s_scr[...] = jnp.zeros_like(s_scr)

    lb = _lower_bound(lbl_ref[...])
    hgn = hgn_ref[...]
    ti = lax.broadcasted_iota(I32, (c, c), 0)
    si = lax.broadcasted_iota(I32, (c, c), 1)
    tril = (si <= ti).astype(F32)
    rowl = lax.broadcasted_iota(I32, (c, 1), 0)
    same = {g: (ti // g) == (si // g) for g in (32, 16, 8)}
    diag_mask = same[8] & (si <= ti)

    for ci, bi in ((ci, bi) for ci in range(z_ref.shape[1] // c) for bi in range(nb)):
        rows = slice(ci * c, (ci + 1) * c)
        z = z_ref[bi, rows, :]
        hq, hf, hi, hog = (z[:, k * HG_W:(k + 1) * HG_W] for k in range(4))
        logf, kk = _forget(hf, lb)
        b = _dot(tril, logf, precision=lax.Precision.HIGHEST)

        qs, ks = [], []
        for m in (32, 16, 8):
            upper = (rowl % (2 * m)) >= m
            e = jnp.exp(-jnp.abs(b - _group_row(b, 2 * m, m - 1)))
            qs.append(jnp.where(upper, hq * e, 0.0).astype(BF16))
            ks.append(jnp.where(upper, 0.0, kk * e).astype(BF16))
        mid = 0.5 * (_group_row(b - logf, 8, 0) + _group_row(b, 8, 7))
        qs.append((hq * jnp.exp(b - mid)).astype(BF16))
        ks.append((kk * jnp.exp(mid - b)).astype(BF16))

        qe = (hq * jnp.exp(b)).astype(BF16)
        bend = b[c - 1:c, :]
        kdec = (kk * jnp.exp(bend - b)).astype(BF16)
        ebend = jnp.exp(bend)
        vb = hi.astype(BF16)

        for h in range(HG_HEADS):
            sl = slice(h * HG_D, (h + 1) * HG_D)
            p32, p16, p8, pd = (_dot_nt(q[:, sl], k[:, sl]) for q, k in zip(qs, ks))
            a = (p32 + jnp.where(same[32], p16, 0.0) + jnp.where(same[16], p8, 0.0)
                 + jnp.where(diag_mask, pd, 0.0))
            st = s_scr[bi, h]
            o = _dot_nt(qe[:, sl], st.astype(BF16)) + _dot(a.astype(BF16), vb[:, sl])
            st_new = st * ebend[:, sl] + _dot_tn(vb[:, sl], kdec[:, sl])
            s_scr[bi, h] = st_new
            o_ref[bi, rows, sl] = _hg_out(o, hog[:, sl], hgn)

    @pl.when(i == pl.num_programs(0) - 1)
    def _():
        for bi in range(nb):
            for h in range(HG_HEADS):
                st_ref[bi, h] = s_scr[bi, h].T


def _hgrn_prompt(lbl, hgn, zh):
    nb, t, _ = zh.shape
    c = min(HG_CHUNK * HG_STEP_CHUNKS, t)
    assert t % c == 0 and c % HG_CHUNK == 0
    return pl.pallas_call(
        _hgrn_prompt_kernel,
        grid=(t // c,),
        in_specs=[pl.BlockSpec(lbl.shape, lambda i: (0, 0)),
                  pl.BlockSpec(hgn.shape, lambda i: (0, 0)),
                  pl.BlockSpec((nb, c, 4 * HG_W), lambda i: (0, i, 0))],
        out_specs=[pl.BlockSpec((nb, c, HG_W), lambda i: (0, i, 0)),
                   pl.BlockSpec((nb, HG_HEADS, HG_D, HG_D), lambda i: (0, 0, 0, 0))],
        out_shape=[jax.ShapeDtypeStruct((nb, t, HG_W), BF16),
                   jax.ShapeDtypeStruct((nb, HG_HEADS, HG_D, HG_D), F32)],
        scratch_shapes=[pltpu.VMEM((nb, HG_HEADS, HG_D, HG_D), F32)],
        compiler_params=_cparams(("arbitrary",)),
        name="hgrn_prompt",
    )(lbl, hgn, zh)


def _hgrn_sample_kernel(lbl_ref, hgn_ref, z_ref, s0_ref, o_ref, st_ref, *, t):
    rows = z_ref.shape[0]
    nseq = rows // t
    lb = _lower_bound(lbl_ref[...])
    hgn = hgn_ref[...]
    z = z_ref[...]
    hq, hf, hi, hog = (z[:, k * HG_W:(k + 1) * HG_W] for k in range(4))
    logf, kk = _forget(hf, lb)
    tl = lax.broadcasted_iota(I32, (rows, 1), 0) % t

    li = lax.broadcasted_iota(I32, (HG_W, HG_W), 0) // HG_D
    lj = lax.broadcasted_iota(I32, (HG_W, HG_W), 1) // HG_D
    head_ones = (li == lj).astype(BF16)

    shift = lambda x, n: pltpu.roll(x, n % rows, axis=0)
    dsum = jnp.zeros_like(logf)
    b = jnp.zeros_like(logf)
    suf = jnp.zeros_like(logf)
    o_intra = jnp.zeros_like(logf)
    for delta in range(t):
        ok = tl >= delta
        if delta > 0:
            dsum = dsum + shift(logf, delta - 1)
            suf = suf + jnp.where(tl + delta < t, shift(logf, -delta), 0.0)
        b = b + jnp.where(ok, shift(logf, delta), 0.0)
        term = jnp.where(ok, hq * shift(kk, delta) * jnp.exp(dsum), 0.0)
        a = _dot(term.astype(BF16), head_ones)
        o_intra = o_intra + a * shift(hi, delta)

    qe = (hq * jnp.exp(b)).astype(BF16)
    kdec = kk * jnp.exp(suf)
    vb = hi.astype(BF16)
    eb = jnp.exp(b + suf)

    seq_of_row = lax.broadcasted_iota(I32, (rows, 1), 0) // t
    for h in range(HG_HEADS):
        sl = slice(h * HG_D, (h + 1) * HG_D)
        eb_t = eb[:, sl].T
        o = o_intra[:, sl]
        for s in range(nseq):
            mine = seq_of_row == s
            s0 = s0_ref[s, h]
            o = o + jnp.where(mine, _dot(qe[:, sl], s0.astype(BF16)), 0.0)
            kd = jnp.where(mine, kdec[:, sl], 0.0).astype(BF16)
            st_ref[s, h] = s0 * eb_t[:, s * t:s * t + 1] + _dot_tn(kd, vb[:, sl])
        o_ref[:, sl] = _hg_out(o, hog[:, sl], hgn)


def _hgrn_sample(lbl, hgn, zh, s0, t, seq_blk):
    rows = zh.shape[0]
    nseq = rows // t
    rb = seq_blk * t
    return pl.pallas_call(
        functools.partial(_hgrn_sample_kernel, t=t),
        grid=(nseq // seq_blk,),
        in_specs=[pl.BlockSpec(lbl.shape, lambda i: (0, 0)),
                  pl.BlockSpec(hgn.shape, lambda i: (0, 0)),
                  pl.BlockSpec((rb, 4 * HG_W), lambda i: (i, 0)),
                  pl.BlockSpec((seq_blk, HG_HEADS, HG_D, HG_D), lambda i: (i, 0, 0, 0))],
        out_specs=[pl.BlockSpec((rb, HG_W), lambda i: (i, 0)),
                   pl.BlockSpec((seq_blk, HG_HEADS, HG_D, HG_D), lambda i: (i, 0, 0, 0))],
        out_shape=[jax.ShapeDtypeStruct((rows, HG_W), BF16),
                   jax.ShapeDtypeStruct(s0.shape, F32)],
        compiler_params=_cparams(("arbitrary",)),
        name="hgrn_sample",
    )(lbl, hgn, zh, s0)


def _key_to_float(key):
    bits = key ^ ((key >> 31) & 0x7FFFFFFF)
    return lax.bitcast_convert_type(bits, F32)


def _bit_search(accept, nbits, start, npass=None):
    def body(k, x):
        cand = x | lax.shift_left(jnp.int32(1), jnp.asarray(nbits - 1 - k, I32))
        return jnp.where(accept(cand), cand, x)
    return lax.fori_loop(0, nbits if npass is None else npass, body, start)


def _any(x):
    return jnp.max(jnp.where(x, 1.0, 0.0)) > 0.5


def _select(need, count, masked_min, nbits_idx):
    shape = need.shape
    ge = lambda x: count(lambda s, kp, x: s >= x, x)
    enough = lambda k: ge(_key_to_float(k)) >= need
    start = jnp.where(ge(jnp.zeros(shape, F32)) >= need, 0, INT_MIN).astype(I32)
    tkey = _bit_search(enough, 31, start, SEARCH_BITS - 1)

    in_set = lambda s, kp, tf, lo: (s >= tf) & (s > lo)
    droppable = lambda extra, cm: (extra > 0) & (extra >= cm)

    def drop_minima(tkey, max_trips):
        tf = _key_to_float(tkey)

        def stats(lo):
            m = masked_min(in_set, tf, lo)
            return m, count(lambda s, kp, m: s == m, m)

        def drop(st):
            n, lo, extra, m, cm = st
            d = droppable(extra, cm)
            lo = jnp.where(d, m, lo)
            return (n + 1, lo, jnp.where(d, extra - cm, extra)) + stats(lo)

        def more(st):
            pending = _any(droppable(st[2], st[4]))
            return pending if max_trips is None else pending & (st[0] < max_trips)

        lo = jnp.full(shape, -jnp.inf, F32)
        return lax.while_loop(more, drop, (jnp.int32(0), lo, ge(tf) - need) + stats(lo))[2:]

    extra, m, cm = drop_minima(tkey, MAX_DROPS)
    extra, m, cm = lax.cond(
        _any(droppable(extra, cm)),
        lambda: drop_minima(_bit_search(enough, 32 - SEARCH_BITS, tkey), None),
        lambda: (extra, m, cm))
    keep = cm - extra
    ties_before = lambda x: count(lambda s, kp, m, x: (s == m) & (kp < x), m, x)
    cut = lax.cond(
        _any(extra > 0),
        lambda: _bit_search(lambda x: ties_before(x) < keep, nbits_idx, jnp.zeros(shape, I32)),
        lambda: jnp.full(shape, 2 ** nbits_idx - 1, I32))
    return m, cut


def _attn_prompt_kernel(aq_ref, iq_ref, iw_ref, kb_ref, vb_ref, ikb_ref, o_ref,
                        sc_scr, w_scr, iq_scr, lg_scr, *, n_sel):
    qb = aq_ref.shape[0]
    kg = sc_scr.shape[2]
    nc = kg // LANES
    i = pl.program_id(1)
    ng = ((i + 1) * qb + kg - 1) // kg
    qpos = i * qb + lax.broadcasted_iota(I32, (qb, 1), 0)
    lane = lax.broadcasted_iota(I32, (1, LANES), 1)
    kpos_of = lambda g, c: g * kg + c * LANES + lane
    rows_of = lambda g: pl.ds(pl.multiple_of(g * kg, kg), kg)

    iw = iw_ref[...]
    for h in range(IDX_HEADS):
        w_scr[h] = jnp.broadcast_to(iw[:, h:h + 1], (qb, LANES))
        iq_scr[h] = iq_ref[:, h * IDX_DIM:(h + 1) * IDX_DIM]

    def sweep(body, init, n):
        def trip(j, c):
            for k in range(ATT_UNROLL):
                c = body(j * ATT_UNROLL + k, c)
            return c
        c = lax.fori_loop(0, n // ATT_UNROLL, trip, init)
        for k in range(ATT_UNROLL - 1, 0, -1):
            c = lax.cond(n % ATT_UNROLL >= k, lambda c, k=k: body(n - k, c), lambda c: c, c)
        return c

    nsub = qb // SUB_ROWS
    sub_rows = [slice(u * SUB_ROWS, (u + 1) * SUB_ROWS) for u in range(nsub)]
    sub_ng = [(i * qb + (u + 1) * SUB_ROWS + kg - 1) // kg for u in range(nsub)]

    for rs, ng_u in zip(sub_rows, sub_ng):
        def score_group(g, carry, rs=rs):
            ikt = ikb_ref[rows_of(g), :]
            acc = [jnp.zeros((SUB_ROWS, LANES), F32) for _ in range(nc)]
            for h in range(IDX_HEADS):
                d = jnp.maximum(_dot_nt(iq_scr[h, rs, :], ikt), 0.0)
                w = w_scr[h, rs, :]
                for c in range(nc):
                    acc[c] = acc[c] + d[:, c * LANES:(c + 1) * LANES] * w
            for c in range(nc):
                cs = slice(c * LANES, (c + 1) * LANES)
                sc_scr[g, rs, cs] = jnp.where(kpos_of(g, c) <= qpos[rs], acc[c] * IDX_SCALE,
                                              -jnp.inf)
            return carry

        def no_keys(g, carry, rs=rs):
            sc_scr[g, rs, :] = jnp.full((SUB_ROWS, kg), -jnp.inf, F32)
            return carry

        sweep(score_group, 0, ng_u)
        lax.fori_loop(ng_u, ng, no_keys, 0)

    def reduce_keys(pred, ops, init, elem, combine, lane_reduce):
        tiled = [jnp.broadcast_to(x, (qb, LANES)) for x in ops]
        accs = []
        for rs in sub_rows:
            def body(g, acc, rs=rs, mine=[x[rs] for x in tiled]):
                for c in range(nc):
                    s = sc_scr[g, rs, c * LANES:(c + 1) * LANES]
                    acc = combine(acc, elem(pred(s, kpos_of(g, c), *mine), s))
                return acc
            accs.append(lax.fori_loop(0, ng, body, jnp.full((SUB_ROWS, LANES), init, F32)))
        return lane_reduce(jnp.concatenate(accs, axis=0), axis=1, keepdims=True)

    count = lambda pred, *ops: reduce_keys(pred, ops, 0.0, lambda p, s: jnp.where(p, 1.0, 0.0),
                                           lambda a, b: a + b, jnp.sum)
    masked_min = lambda pred, *ops: reduce_keys(
        pred, ops, jnp.inf, lambda p, s: jnp.where(p, s, jnp.inf), jnp.minimum, jnp.min)
    need = jnp.minimum(qpos + 1, n_sel).astype(F32)
    m, cut = _select(need, count, masked_min, max(1, (kb_ref.shape[0] - 1).bit_length()))

    def bias_group(g, carry):
        for c in range(nc):
            cs = slice(c * LANES, (c + 1) * LANES)
            s = sc_scr[g, :, cs]
            sel = (s > m) | ((s == m) & (kpos_of(g, c) <= cut))
            sc_scr[g, :, cs] = jnp.where(sel, 0.0, NEG_BIG)
        return carry

    lax.fori_loop(0, ng, bias_group, 0)

    scale = AT_DH ** -0.5
    heads = [slice(h * AT_DH, (h + 1) * AT_DH) for h in range(AT_HEADS)]

    for rs, ng_u in zip(sub_rows, sub_ng):
        def logits_group(g, mrun, rs=rs):
            bias = sc_scr[g, rs, :]
            out = []
            for h, sl in enumerate(heads):
                t = _dot_nt(aq_ref[rs, sl], kb_ref[rows_of(g), sl]) * scale + bias
                lg_scr[h, g] = t
                mt = mrun[h]
                for c in range(nc):
                    mt = jnp.maximum(mt, t[:, c * LANES:(c + 1) * LANES])
                out.append(mt)
            return tuple(out)

        mrun = sweep(logits_group,
                     tuple(jnp.full((SUB_ROWS, LANES), NEG_BIG, F32) for _ in heads), ng_u)
        for h in range(AT_HEADS):
            w_scr[h, rs, :] = jnp.broadcast_to(jnp.max(mrun[h], axis=1, keepdims=True),
                                               (SUB_ROWS, LANES))

        def pv_group(g, carry, rs=rs):
            out = []
            for h, (sl, (lsum, acc)) in enumerate(zip(heads, carry)):
                mx = w_scr[h, rs, :]
                ps = [jnp.exp(lg_scr[h, g, :, c * LANES:(c + 1) * LANES] - mx)
                      for c in range(nc)]
                for p in ps:
                    lsum = lsum + p
                pb = jnp.concatenate([p.astype(BF16) for p in ps], axis=1)
                out.append((lsum, acc + _dot(pb, vb_ref[rows_of(g), sl])))
            return tuple(out)

        zeros = jnp.zeros((SUB_ROWS, LANES), F32)
        stats = sweep(pv_group, tuple((zeros, jnp.zeros((SUB_ROWS, AT_DH), F32)) for _ in heads),
                      ng_u)
        for sl, (lsum, acc) in zip(heads, stats):
            o_ref[rs, sl] = (acc / jnp.sum(lsum, axis=1, keepdims=True)).astype(BF16)


def _attn_prompt(aq, iq, iw, kb, vb, ikb, nb, qb):
    rows = aq.shape[0]
    t = rows // nb
    n_sel = min(TOPK_MAX, t // 4)
    nq = t // qb
    kg = min(KEY_GROUP, t)
    assert t % kg == 0 and kg % LANES == 0 and t % qb == 0 and qb % SUB_ROWS == 0
    qrow = lambda n: pl.BlockSpec((qb, n), lambda b, i: (b * nq + i, 0))
    seq = lambda n: pl.BlockSpec((t, n), lambda b, i: (b, 0), pipeline_mode=pl.Buffered(1))
    return pl.pallas_call(
        functools.partial(_attn_prompt_kernel, n_sel=n_sel),
        grid=(nb, nq),
        in_specs=[qrow(AT_W), qrow(IDX_W), qrow(IDX_HEADS), seq(AT_W), seq(AT_W), seq(IDX_DIM)],
        out_specs=qrow(AT_W),
        out_shape=jax.ShapeDtypeStruct((rows, AT_W), BF16),
        scratch_shapes=[pltpu.VMEM((t // kg, qb, kg), F32),
                        pltpu.VMEM((IDX_HEADS, qb, LANES), F32),
                        pltpu.VMEM((IDX_HEADS, qb, IDX_DIM), BF16),
                        pltpu.VMEM((AT_HEADS, t // kg, SUB_ROWS, kg), F32)],
        compiler_params=_cparams(("arbitrary", "arbitrary")),
        name="attn_prompt",
    )(aq, iq, iw, kb, vb, ikb)


def _pad_rows(x, rows):
    return jnp.concatenate([x, jnp.zeros((rows - x.shape[0], x.shape[1]), x.dtype)], axis=0)


def _select_sample_kernel(pt_ref, iq_ref, iw_ref, ikn_ref, *rest, n_pages, n_sel, t, seq_blk):
    del pt_ref
    ik_pages = rest[:n_pages]
    bias_ref, sc_scr = rest[n_pages:]
    page = ik_pages[0].shape[1]
    past = n_pages * page
    total = past + page
    nc = total // LANES
    slot = pl.program_id(0) % seq_blk
    rows = pl.ds(pl.multiple_of(slot * t, t), t)
    lane = lax.broadcasted_iota(I32, (1, LANES), 1)

    iq = iq_ref[...]
    w = iw_ref[...]

    def head_sum(d):
        r = jnp.maximum(d, 0.0) * w
        return jnp.sum(r.reshape(t, IDX_HEADS, d.shape[1]), axis=1) * IDX_SCALE

    for p in range(n_pages):
        sc_scr[rows, p * page:(p + 1) * page] = head_sum(_dot(iq, ik_pages[p][...].astype(BF16)))
    new = head_sum(_dot_nt(iq, _pad_rows(ikn_ref[...], page).astype(BF16)))
    tok_pos = lax.broadcasted_iota(I32, (t, 1), 0)
    sc_scr[rows, past:total] = jnp.where(lane <= tok_pos, new, -jnp.inf)

    @pl.when(slot == seq_blk - 1)
    def _():
        nrow = seq_blk * t
        qpos = past + lax.broadcasted_iota(I32, (nrow, 1), 0) % t
        kpos_of = lambda c: c * LANES + lane

        def reduce_keys(pred, ops, init, elem, combine, lane_reduce):
            tiled = [jnp.broadcast_to(x, (nrow, LANES)) for x in ops]
            acc = jnp.full((nrow, LANES), init, F32)
            for c in range(nc):
                s = sc_scr[:, c * LANES:(c + 1) * LANES]
                acc = combine(acc, elem(pred(s, kpos_of(c), *tiled), s))
            return lane_reduce(acc, axis=1, keepdims=True)

        count = lambda pred, *ops: reduce_keys(
            pred, ops, 0.0, lambda p, s: jnp.where(p, 1.0, 0.0), lambda a, b: a + b, jnp.sum)
        masked_min = lambda pred, *ops: reduce_keys(
            pred, ops, jnp.inf, lambda p, s: jnp.where(p, s, jnp.inf), jnp.minimum, jnp.min)
        need = jnp.minimum(qpos + 1, n_sel).astype(F32)
        m, cut = _select(need, count, masked_min, max(1, (total - 1).bit_length()))
        for c in range(nc):
            s = sc_scr[:, c * LANES:(c + 1) * LANES]
            sel = (s > m) | ((s == m) & (kpos_of(c) <= cut))
            bias_ref[:, :, c * LANES:(c + 1) * LANES] = jnp.where(sel, 0.0, NEG_BIG).reshape(
                seq_blk, t, LANES)


def _attend_sample_kernel(pt_ref, aq_ref, bias_ref, kn_ref, vn_ref, *rest, n_pages, t):
    del pt_ref
    k_pages = rest[:n_pages]
    v_pages = rest[n_pages:2 * n_pages]
    o_ref, kb_scr, vb_scr = rest[2 * n_pages:]
    page = k_pages[0].shape[0] // AT_HEADS
    past = n_pages * page
    total = past + page

    for p in range(n_pages):
        rows = slice(p * page, (p + 1) * page)
        for h in range(AT_HEADS):
            head_rows = pl.ds(h, page, stride=AT_HEADS)
            kb_scr[h, rows, :] = k_pages[p][head_rows, :].astype(BF16)
            vb_scr[h, rows, :] = v_pages[p][head_rows, :].astype(BF16)
    tail = slice(past, total)
    for h in range(AT_HEADS):
        sl = slice(h * AT_DH, (h + 1) * AT_DH)
        kb_scr[h, tail, :] = _pad_rows(kn_ref[:, sl], page).astype(BF16)
        vb_scr[h, tail, :] = _pad_rows(vn_ref[:, sl], page).astype(BF16)

    scale = AT_DH ** -0.5
    aq = aq_ref[...].astype(BF16)
    bias = bias_ref[...]
    for h in range(AT_HEADS):
        sl = slice(h * AT_DH, (h + 1) * AT_DH)
        s = _dot_nt(aq[:, sl], kb_scr[h]) * scale + bias
        p = jnp.exp(s - jnp.max(s, axis=1, keepdims=True))
        l = jnp.sum(p, axis=1, keepdims=True)
        o_ref[:, sl] = _dot(p.astype(BF16), vb_scr[h]) / l


def _attn_sample(page_table, aq, iq, iw, kn, vn, ikn, cache_ikt, cache_k, cache_v, t_real):
    nseq, n_pages = page_table.shape
    page = cache_ikt.shape[2]
    past = n_pages * page
    total = past + page
    t = aq.shape[1]
    n_sel = min(TOPK_MAX, (past + t_real) // 4)
    seq_blk = min(SELECT_SEQS, nseq)
    assert nseq % seq_blk == 0
    tok = lambda r, n: pl.BlockSpec((None, r, n), lambda s, pt: (s, 0, 0))
    pg = lambda r, n, p: pl.BlockSpec((None, r, n), lambda s, pt, p=p: (pt[s, p], 0, 0))

    bias = pl.pallas_call(
        functools.partial(_select_sample_kernel, n_pages=n_pages, n_sel=n_sel, t=t,
                          seq_blk=seq_blk),
        grid_spec=pltpu.PrefetchScalarGridSpec(
            num_scalar_prefetch=1,
            grid=(nseq,),
            in_specs=[tok(t * IDX_HEADS, IDX_DIM), tok(t * IDX_HEADS, 1), tok(t, IDX_DIM)]
            + [pg(IDX_DIM, page, p) for p in range(n_pages)],
            out_specs=pl.BlockSpec((seq_blk, t, total), lambda s, pt: (s // seq_blk, 0, 0)),
            scratch_shapes=[pltpu.VMEM((seq_blk * t, total), F32)]),
        out_shape=jax.ShapeDtypeStruct((nseq, t, total), F32),
        compiler_params=_cparams(("arbitrary",)),
        name="select_sample",
    )(page_table, iq, iw, ikn, *([cache_ikt] * n_pages))

    return pl.pallas_call(
        functools.partial(_attend_sample_kernel, n_pages=n_pages, t=t),
        grid_spec=pltpu.PrefetchScalarGridSpec(
            num_scalar_prefetch=1,
            grid=(nseq,),
            in_specs=[tok(t, AT_W), tok(t, total), tok(t, AT_W), tok(t, AT_W)]
            + [pg(page * AT_HEADS, AT_DH, p) for p in range(n_pages)] * 2,
            out_specs=tok(t, AT_W),
            scratch_shapes=[pltpu.VMEM((AT_HEADS, total, AT_DH), BF16),
                            pltpu.VMEM((AT_HEADS, total, AT_DH), BF16)]),
        out_shape=jax.ShapeDtypeStruct((nseq, t, AT_W), F32),
        compiler_params=_cparams(("arbitrary",)),
        name="attend_sample",
    )(page_table, aq, bias, kn, vn, *([cache_k] * n_pages), *([cache_v] * n_pages))


def _out_kernel(x_ref, og_ref, ao_ref, gg_ref, wa_ref, wb_ref, wo_ref, n2_ref, wu_ref, wd_ref,
                nf_ref, y_ref):
    d = x_ref.shape[1]
    ya = _dot(og_ref[...], wa_ref[...])
    yb = _dot(ao_ref[...], wb_ref[...])
    m = _sigmoid(gg_ref[:, :d]) * ya + _sigmoid(gg_ref[:, d:]) * yb
    x1 = x_ref[...] + _dot(m.astype(BF16), wo_ref[...])
    h2 = _rms(x1, n2_ref[...]).astype(BF16)
    acc = x1
    ff = wu_ref.shape[1]
    for c in range(ff // d):
        cs = slice(c * d, (c + 1) * d)
        u = jnp.maximum(_dot(h2, wu_ref[:, cs]), 0.0)
        acc = acc + _dot((u * u).astype(BF16), wd_ref[cs, :])
    y_ref[...] = _rms(acc, nf_ref[...])


def _out_mlp(x, og, ao, gg, wa, wb, wo, n2, wu, wd, nf, tm):
    rows, d = x.shape
    row = lambda n: pl.BlockSpec((tm, n), lambda i: (i, 0))
    const = lambda a: pl.BlockSpec(a.shape, lambda i: (0, 0), pipeline_mode=pl.Buffered(1))
    return pl.pallas_call(
        _out_kernel,
        grid=(rows // tm,),
        in_specs=[row(d), row(HG_W), row(AT_W), row(2 * d), const(wa), const(wb), const(wo),
                  const(n2), const(wu), const(wd), const(nf)],
        out_specs=row(d),
        out_shape=jax.ShapeDtypeStruct((rows, d), F32),
        compiler_params=_cparams(("arbitrary",)),
        name="out_mlp",
    )(x, og, ao, gg, wa, wb, wo, n2, wu, wd, nf)


def _pack_w_in(w):
    d = w.shape[0]
    pad = lambda n: jnp.zeros((d, n), w.dtype)
    ik0 = _C_IK
    iw0 = ik0 + IDX_DIM
    gg0 = iw0 + IDX_HEADS
    return jnp.concatenate(
        [w[:, :ik0], w[:, ik0:iw0], pad(LANES - IDX_DIM), w[:, iw0:gg0], pad(LANES - IDX_HEADS),
         w[:, gg0:]], axis=1).astype(BF16)


def kernel(x_prompt, x_sample, cache_k, cache_v, cache_idx_k, state_hgrn, page_table, lb_logits,
           w_in, hg_norm, w_a, w_b, w_o, norm1, norm2, w_up, w_down, norm_f):
    depth = w_in.shape[0]
    assert depth == 1, "single-layer stack"
    nb, t, d = x_prompt.shape
    ns, ts, _ = x_sample.shape
    n_phys, page = cache_k.shape[1], cache_k.shape[2]

    w_in_p = _pack_w_in(w_in[0])
    assert w_in_p.shape[1] == _C_END
    bf = lambda a: a.astype(BF16)
    wa, wb, wo, wu, wd = bf(w_a[0]), bf(w_b[0]), bf(w_o[0]), bf(w_up[0]), bf(w_down[0])
    n1, n2, nf = norm1[0][None], norm2[0][None], norm_f[None]
    hgn = hg_norm[0][None]

    xp = x_prompt.reshape(nb * t, d)
    xs = x_sample.reshape(ns * ts, d)
    tm_p, tm_s = min(ROW_TILE, nb * t), min(ROW_TILE, ns * ts)
    zh_p, aq_p, ak_p, av_p, akb_p, avb_p, iq_p, ik_p, ikb_p, iw_p, gg_p = _inproj(xp, n1, w_in_p, tm_p)
    zh_s, aq_s, ak_s, av_s, _, _, iq_s, ik_s, _, iw_s, gg_s = _inproj(xs, n1, w_in_p, tm_s)

    og_p, st_p = _hgrn_prompt(lb_logits, hgn, zh_p.reshape(nb, t, 4 * HG_W))
    og_s, st_s = _hgrn_sample(lb_logits, hgn, zh_s, state_hgrn[0], ts, 8)

    ao_p = _attn_prompt(aq_p, iq_p, iw_p, akb_p, avb_p, ikb_p, nb, min(SELECT_ROWS, t))

    tp = -(-ts // 8) * 8
    tok = lambda a: jnp.pad(a.reshape(ns, ts, -1), ((0, 0), (0, tp - ts), (0, 0)))
    ao_s = _attn_sample(
        page_table,
        tok(aq_s.astype(F32)),
        tok(iq_s).reshape(ns, tp * IDX_HEADS, IDX_DIM),
        tok(iw_s).reshape(ns, tp * IDX_HEADS, 1),
        tok(ak_s), tok(av_s), tok(ik_s),
        jnp.swapaxes(cache_idx_k, 2, 3).reshape(n_phys, IDX_DIM, page),
        cache_k.reshape(n_phys, page * AT_HEADS, AT_DH), cache_v.reshape(n_phys, page * AT_HEADS, AT_DH),
        ts)
    ao_s = ao_s[:, :ts].reshape(ns * ts, AT_W).astype(BF16)

    y_p = _out_mlp(xp, og_p.reshape(nb * t, HG_W), ao_p, gg_p, wa, wb, wo, n2, wu, wd, nf, tm_p)
    y_s = _out_mlp(xs, og_s, ao_s, gg_s, wa, wb, wo, n2, wu, wd, nf, tm_s)

    return (y_p.reshape(nb, t, d), y_s.reshape(ns, ts, d),
            ak_p.reshape(1, nb, t, AT_HEADS, AT_DH), av_p.reshape(1, nb, t, AT_HEADS, AT_DH),
            ik_p.reshape(1, nb, t, IDX_DIM), st_p[None],
            ak_s.reshape(1, ns, ts, AT_HEADS, AT_DH), av_s.reshape(1, ns, ts, AT_HEADS, AT_DH),
            ik_s.reshape(1, ns, ts, IDX_DIM), st_s[None])
```

```python
import functools

import jax
import jax.numpy as jnp
from jax import lax
from jax.experimental import pallas as pl
from jax.experimental.pallas import tpu as pltpu

F32 = jnp.float32
BF16 = jnp.bfloat16
I32 = jnp.int32

EPS = 1e-6
HG_HEADS = 4
HG_D = 128
HG_W = HG_HEADS * HG_D
HG_CHUNK = 64
HG_STEP_CHUNKS = 4
AT_HEADS = 4
AT_DH = 128
AT_W = AT_HEADS * AT_DH
IDX_HEADS = 8
IDX_DIM = 64
IDX_W = IDX_HEADS * IDX_DIM
IDX_SCALE = (IDX_HEADS * IDX_DIM) ** -0.5
TOPK_MAX = 256
LANES = 128
INT_MIN = -(2 ** 31)
NEG_BIG = -1e30
VMEM_LIMIT = 56 * 1024 * 1024
ROW_TILE = 512
KEY_GROUP = 512
SEARCH_BITS = 23
MAX_DROPS = 2
SELECT_SEQS = 16
SELECT_ROWS = 256
SUB_ROWS = 128
ATT_UNROLL = 4

_C_ZH = 0
_C_AQ = 2048
_C_AK = 2560
_C_AV = 3072
_C_IQ = 3584
_C_IK = 4096
_C_IW = 4224
_C_GG = 4352
_C_END = 6400


def _dot(a, b, **kw):
    return jnp.dot(a, b, preferred_element_type=F32, **kw)


def _dot_nt(a, b):
    return lax.dot_general(a, b, (((1,), (1,)), ((), ())), preferred_element_type=F32)


def _dot_tn(a, b):
    return lax.dot_general(a, b, (((0,), (0,)), ((), ())), preferred_element_type=F32)


def _rms(x, g):
    return x * lax.rsqrt(jnp.mean(x * x, axis=-1, keepdims=True) + EPS) * g


def _sigmoid(x):
    return 1.0 / (1.0 + jnp.exp(-x))


def _cparams(sem):
    return pltpu.CompilerParams(dimension_semantics=sem, vmem_limit_bytes=VMEM_LIMIT)


def _inproj_kernel(x_ref, g_ref, w_ref, zh_ref, aq_ref, ak_ref, av_ref, akb_ref, avb_ref,
                   iq_ref, ik_ref, ikb_ref, iw_ref, gg_ref):
    h = _rms(x_ref[...], g_ref[...]).astype(BF16)

    def mm(a, n):
        return _dot(h, w_ref[:, a:a + n])

    for c in range(4):
        zh_ref[:, c * 512:(c + 1) * 512] = mm(_C_ZH + c * 512, 512)
    aq_ref[...] = mm(_C_AQ, AT_W).astype(BF16)
    tm = x_ref.shape[0]
    for c0, f_ref, b_ref in ((_C_AK, ak_ref, akb_ref), (_C_AV, av_ref, avb_ref)):
        kv = mm(c0, AT_W)
        b_ref[...] = kv.astype(BF16)
        for hd in range(AT_HEADS):
            f_ref[pl.ds(hd, tm, stride=AT_HEADS), :] = kv[:, hd * AT_DH:(hd + 1) * AT_DH]
    iq_ref[...] = mm(_C_IQ, IDX_W).astype(BF16)
    ik = mm(_C_IK, LANES)[:, :IDX_DIM]
    ik_ref[...] = ik
    ikb_ref[...] = ik.astype(BF16)
    iw_ref[...] = mm(_C_IW, LANES)[:, :IDX_HEADS]
    for c in range(4):
        gg_ref[:, c * 512:(c + 1) * 512] = mm(_C_GG + c * 512, 512)


def _inproj(x, g, w, tm):
    rows, d = x.shape
    row = lambda n, r=1: pl.BlockSpec((tm * r, n), lambda i: (i, 0))
    outs = [(2048, F32, 1), (AT_W, BF16, 1), (AT_DH, F32, AT_HEADS), (AT_DH, F32, AT_HEADS),
            (AT_W, BF16, 1), (AT_W, BF16, 1), (IDX_W, BF16, 1), (IDX_DIM, F32, 1),
            (IDX_DIM, BF16, 1), (IDX_HEADS, F32, 1), (2048, F32, 1)]
    return pl.pallas_call(
        _inproj_kernel,
        grid=(rows // tm,),
        in_specs=[row(d), pl.BlockSpec((1, d), lambda i: (0, 0)),
                  pl.BlockSpec(w.shape, lambda i: (0, 0), pipeline_mode=pl.Buffered(1))],
        out_specs=[row(n, r) for n, _, r in outs],
        out_shape=[jax.ShapeDtypeStruct((rows * r, n), t) for n, t, r in outs],
        compiler_params=_cparams(("arbitrary",)),
        name="inproj",
    )(x, g, w)


def _lower_bound(lbl):
    e = jnp.exp(lbl - jnp.max(lbl, axis=0, keepdims=True))
    return e[0:1] / jnp.sum(e, axis=0, keepdims=True)


def _forget(hf, lb):
    f = lb + (1.0 - lb) * _sigmoid(hf)
    return jnp.log(f), 1.0 - f


def _group_row(x, gsize, r):
    n, w = x.shape
    g = x.reshape(n // gsize, gsize, w)
    return jnp.broadcast_to(g[:, r:r + 1, :], g.shape).reshape(n, w)


def _hg_out(o, hog, hgn):
    return (_rms(o, hgn) * (hog * _sigmoid(hog))).astype(BF16)


def _hgrn_prompt_kernel(lbl_ref, hgn_ref, z_ref, o_ref, st_ref, s_scr):
    i = pl.program_id(0)
    nb = z_ref.shape[0]
    c = HG_CHUNK

    @pl.when(i == 0)
    def _():
        s_scr[...] = jnp.zeros_like(s_scr)

    lb = _lower_bound(lbl_ref[...])
    hgn = hgn_ref[...]
    ti = lax.broadcasted_iota(I32, (c, c), 0)
    si = lax.broadcasted_iota(I32, (c, c), 1)
    tril = (si <= ti).astype(F32)
    rowl = lax.broadcasted_iota(I32, (c, 1), 0)
    same = {g: (ti // g) == (si // g) for g in (32, 16, 8)}
    diag_mask = same[8] & (si <= ti)

    for ci, bi in ((ci, bi) for ci in range(z_ref.shape[1] // c) for bi in range(nb)):
        rows = slice(ci * c, (ci + 1) * c)
        z = z_ref[bi, rows, :]
        hq, hf, hi, hog = (z[:, k * HG_W:(k + 1) * HG_W] for k in range(4))
        logf, kk = _forget(hf, lb)
        b = _dot(tril, logf, precision=lax.Precision.HIGHEST)

        qs, ks = [], []
        for m in (32, 16, 8):
            upper = (rowl % (2 * m)) >= m
            e = jnp.exp(-jnp.abs(b - _group_row(b, 2 * m, m - 1)))
            qs.append(jnp.where(upper, hq * e, 0.0).astype(BF16))
            ks.append(jnp.where(upper, 0.0, kk * e).astype(BF16))
        mid = 0.5 * (_group_row(b - logf, 8, 0) + _group_row(b, 8, 7))
        qs.append((hq * jnp.exp(b - mid)).astype(BF16))
        ks.append((kk * jnp.exp(mid - b)).astype(BF16))

        qe = (hq * jnp.exp(b)).astype(BF16)
        bend = b[c - 1:c, :]
        kdec = (kk * jnp.exp(bend - b)).astype(BF16)
        ebend = jnp.exp(bend)
        vb = hi.astype(BF16)

        for h in range(HG_HEADS):
            sl = slice(h * HG_D, (h + 1) * HG_D)
            p32, p16, p8, pd = (_dot_nt(q[:, sl], k[:, sl]) for q, k in zip(qs, ks))
            a = (p32 + jnp.where(same[32], p16, 0.0) + jnp.where(same[16], p8, 0.0)
                 + jnp.where(diag_mask, pd, 0.0))
            st = s_scr[bi, h]
            o = _dot_nt(qe[:, sl], st.astype(BF16)) + _dot(a.astype(BF16), vb[:, sl])
            st_new = st * ebend[:, sl] + _dot_tn(vb[:, sl], kdec[:, sl])
            s_scr[bi, h] = st_new
            o_ref[bi, rows, sl] = _hg_out(o, hog[:, sl], hgn)

    @pl.when(i == pl.num_programs(0) - 1)
    def _():
        for bi in range(nb):
            for h in range(HG_HEADS):
                st_ref[bi, h] = s_scr[bi, h].T


def _hgrn_prompt(lbl, hgn, zh):
    nb, t, _ = zh.shape
    c = min(HG_CHUNK * HG_STEP_CHUNKS, t)
    assert t % c == 0 and c % HG_CHUNK == 0
    return pl.pallas_call(
        _hgrn_prompt_kernel,
        grid=(t // c,),
        in_specs=[pl.BlockSpec(lbl.shape, lambda i: (0, 0)),
                  pl.BlockSpec(hgn.shape, lambda i: (0, 0)),
                  pl.BlockSpec((nb, c, 4 * HG_W), lambda i: (0, i, 0))],
        out_specs=[pl.BlockSpec((nb, c, HG_W), lambda i: (0, i, 0)),
                   pl.BlockSpec((nb, HG_HEADS, HG_D, HG_D), lambda i: (0, 0, 0, 0))],
        out_shape=[jax.ShapeDtypeStruct((nb, t, HG_W), BF16),
                   jax.ShapeDtypeStruct((nb, HG_HEADS, HG_D, HG_D), F32)],
        scratch_shapes=[pltpu.VMEM((nb, HG_HEADS, HG_D, HG_D), F32)],
        compiler_params=_cparams(("arbitrary",)),
        name="hgrn_prompt",
    )(lbl, hgn, zh)


def _hgrn_sample_kernel(lbl_ref, hgn_ref, z_ref, s0_ref, o_ref, st_ref, *, t):
    rows = z_ref.shape[0]
    nseq = rows // t
    lb = _lower_bound(lbl_ref[...])
    hgn = hgn_ref[...]
    z = z_ref[...]
    hq, hf, hi, hog = (z[:, k * HG_W:(k + 1) * HG_W] for k in range(4))
    logf, kk = _forget(hf, lb)
    tl = lax.broadcasted_iota(I32, (rows, 1), 0) % t

    li = lax.broadcasted_iota(I32, (HG_W, HG_W), 0) // HG_D
    lj = lax.broadcasted_iota(I32, (HG_W, HG_W), 1) // HG_D
    head_ones = (li == lj).astype(BF16)

    shift = lambda x, n: pltpu.roll(x, n % rows, axis=0)
    dsum = jnp.zeros_like(logf)
    b = jnp.zeros_like(logf)
    suf = jnp.zeros_like(logf)
    o_intra = jnp.zeros_like(logf)
    for delta in range(t):
        ok = tl >= delta
        if delta > 0:
            dsum = dsum + shift(logf, delta - 1)
            suf = suf + jnp.where(tl + delta < t, shift(logf, -delta), 0.0)
        b = b + jnp.where(ok, shift(logf, delta), 0.0)
        term = jnp.where(ok, hq * shift(kk, delta) * jnp.exp(dsum), 0.0)
        a = _dot(term.astype(BF16), head_ones)
        o_intra = o_intra + a * shift(hi, delta)

    qe = (hq * jnp.exp(b)).astype(BF16)
    kdec = kk * jnp.exp(suf)
    vb = hi.astype(BF16)
    eb = jnp.exp(b + suf)

    seq_of_row = lax.broadcasted_iota(I32, (rows, 1), 0) // t
    for h in range(HG_HEADS):
        sl = slice(h * HG_D, (h + 1) * HG_D)
        eb_t = eb[:, sl].T
        o = o_intra[:, sl]
        for s in range(nseq):
            mine = seq_of_row == s
            s0 = s0_ref[s, h]
            o = o + jnp.where(mine, _dot(qe[:, sl], s0.astype(BF16)), 0.0)
            kd = jnp.where(mine, kdec[:, sl], 0.0).astype(BF16)
            st_ref[s, h] = s0 * eb_t[:, s * t:s * t + 1] + _dot_tn(kd, vb[:, sl])
        o_ref[:, sl] = _hg_out(o, hog[:, sl], hgn)


def _hgrn_sample(lbl, hgn, zh, s0, t, seq_blk):
    rows = zh.shape[0]
    nseq = rows // t
    rb = seq_blk * t
    return pl.pallas_call(
        functools.partial(_hgrn_sample_kernel, t=t),
        grid=(nseq // seq_blk,),
        in_specs=[pl.BlockSpec(lbl.shape, lambda i: (0, 0)),
                  pl.BlockSpec(hgn.shape, lambda i: (0, 0)),
                  pl.BlockSpec((rb, 4 * HG_W), lambda i: (i, 0)),
                  pl.BlockSpec((seq_blk, HG_HEADS, HG_D, HG_D), lambda i: (i, 0, 0, 0))],
        out_specs=[pl.BlockSpec((rb, HG_W), lambda i: (i, 0)),
                   pl.BlockSpec((seq_blk, HG_HEADS, HG_D, HG_D), lambda i: (i, 0, 0, 0))],
        out_shape=[jax.ShapeDtypeStruct((rows, HG_W), BF16),
                   jax.ShapeDtypeStruct(s0.shape, F32)],
        compiler_params=_cparams(("arbitrary",)),
        name="hgrn_sample",
    )(lbl, hgn, zh, s0)


def _key_to_float(key):
    bits = key ^ ((key >> 31) & 0x7FFFFFFF)
    return lax.bitcast_convert_type(bits, F32)


def _bit_search(accept, nbits, start, npass=None):
    def body(k, x):
        cand = x | lax.shift_left(jnp.int32(1), jnp.asarray(nbits - 1 - k, I32))
        return jnp.where(accept(cand), cand, x)
    return lax.fori_loop(0, nbits if npass is None else npass, body, start)


def _any(x):
    return jnp.max(jnp.where(x, 1.0, 0.0)) > 0.5


def _select(need, count, masked_min, nbits_idx):
    shape = need.shape
    ge = lambda x: count(lambda s, kp, x: s >= x, x)
    enough = lambda k: ge(_key_to_float(k)) >= need
    start = jnp.where(ge(jnp.zeros(shape, F32)) >= need, 0, INT_MIN).astype(I32)
    tkey = _bit_search(enough, 31, start, SEARCH_BITS - 1)

    in_set = lambda s, kp, tf, lo: (s >= tf) & (s > lo)
    droppable = lambda extra, cm: (extra > 0) & (extra >= cm)

    def drop_minima(tkey, max_trips):
        tf = _key_to_float(tkey)

        def stats(lo):
            m = masked_min(in_set, tf, lo)
            return m, count(lambda s, kp, m: s == m, m)

        def drop(st):
            n, lo, extra, m, cm = st
            d = droppable(extra, cm)
            lo = jnp.where(d, m, lo)
            return (n + 1, lo, jnp.where(d, extra - cm, extra)) + stats(lo)

        def more(st):
            pending = _any(droppable(st[2], st[4]))
            return pending if max_trips is None else pending & (st[0] < max_trips)

        lo = jnp.full(shape, -jnp.inf, F32)
        return lax.while_loop(more, drop, (jnp.int32(0), lo, ge(tf) - need) + stats(lo))[2:]

    extra, m, cm = drop_minima(tkey, MAX_DROPS)
    extra, m, cm = lax.cond(
        _any(droppable(extra, cm)),
        lambda: drop_minima(_bit_search(enough, 32 - SEARCH_BITS, tkey), None),
        lambda: (extra, m, cm))
    keep = cm - extra
    ties_before = lambda x: count(lambda s, kp, m, x: (s == m) & (kp < x), m, x)
    cut = lax.cond(
        _any(extra > 0),
        lambda: _bit_search(lambda x: ties_before(x) < keep, nbits_idx, jnp.zeros(shape, I32)),
        lambda: jnp.full(shape, 2 ** nbits_idx - 1, I32))
    return m, cut


def _attn_prompt_kernel(aq_ref, iq_ref, iw_ref, kb_ref, vb_ref, ikb_ref, o_ref,
                        sc_scr, w_scr, iq_scr, lg_scr, *, n_sel):
    qb = aq_ref.shape[0]
    kg = sc_scr.shape[2]
    nc = kg // LANES
    i = pl.program_id(1)
    ng = ((i + 1) * qb + kg - 1) // kg
    qpos = i * qb + lax.broadcasted_iota(I32, (qb, 1), 0)
    lane = lax.broadcasted_iota(I32, (1, LANES), 1)
    kpos_of = lambda g, c: g * kg + c * LANES + lane
    rows_of = lambda g: pl.ds(pl.multiple_of(g * kg, kg), kg)

    iw = iw_ref[...]
    for h in range(IDX_HEADS):
        w_scr[h] = jnp.broadcast_to(iw[:, h:h + 1], (qb, LANES))
        iq_scr[h] = iq_ref[:, h * IDX_DIM:(h + 1) * IDX_DIM]

    def sweep(body, init, n):
        def trip(j, c):
            for k in range(ATT_UNROLL):
                c = body(j * ATT_UNROLL + k, c)
            return c
        c = lax.fori_loop(0, n // ATT_UNROLL, trip, init)
        for k in range(ATT_UNROLL - 1, 0, -1):
            c = lax.cond(n % ATT_UNROLL >= k, lambda c, k=k: body(n - k, c), lambda c: c, c)
        return c

    nsub = qb // SUB_ROWS
    sub_rows = [slice(u * SUB_ROWS, (u + 1) * SUB_ROWS) for u in range(nsub)]
    sub_ng = [(i * qb + (u + 1) * SUB_ROWS + kg - 1) // kg for u in range(nsub)]

    for rs, ng_u in zip(sub_rows, sub_ng):
        def score_group(g, carry, rs=rs):
            ikt = ikb_ref[rows_of(g), :]
            acc = [jnp.zeros((SUB_ROWS, LANES), F32) for _ in range(nc)]
            for h in range(IDX_HEADS):
                d = jnp.maximum(_dot_nt(iq_scr[h, rs, :], ikt), 0.0)
                w = w_scr[h, rs, :]
                for c in range(nc):
                    acc[c] = acc[c] + d[:, c * LANES:(c + 1) * LANES] * w
            for c in range(nc):
                cs = slice(c * LANES, (c + 1) * LANES)
                sc_scr[g, rs, cs] = jnp.where(kpos_of(g, c) <= qpos[rs], acc[c] * IDX_SCALE,
                                              -jnp.inf)
            return carry

        def no_keys(g, carry, rs=rs):
            sc_scr[g, rs, :] = jnp.full((SUB_ROWS, kg), -jnp.inf, F32)
            return carry

        sweep(score_group, 0, ng_u)
        lax.fori_loop(ng_u, ng, no_keys, 0)

    def reduce_keys(pred, ops, init, elem, combine, lane_reduce):
        tiled = [jnp.broadcast_to(x, (qb, LANES)) for x in ops]
        accs = []
        for rs in sub_rows:
            def body(g, acc, rs=rs, mine=[x[rs] for x in tiled]):
                for c in range(nc):
                    s = sc_scr[g, rs, c * LANES:(c + 1) * LANES]
                    acc = combine(acc, elem(pred(s, kpos_of(g, c), *mine), s))
                return acc
            accs.append(lax.fori_loop(0, ng, body, jnp.full((SUB_ROWS, LANES), init, F32)))
        return lane_reduce(jnp.concatenate(accs, axis=0), axis=1, keepdims=True)

    count = lambda pred, *ops: reduce_keys(pred, ops, 0.0, lambda p, s: jnp.where(p, 1.0, 0.0),
                                           lambda a, b: a + b, jnp.sum)
    masked_min = lambda pred, *ops: reduce_keys(
        pred, ops, jnp.inf, lambda p, s: jnp.where(p, s, jnp.inf), jnp.minimum, jnp.min)
    need = jnp.minimum(qpos + 1, n_sel).astype(F32)
    m, cut = _select(need, count, masked_min, max(1, (kb_ref.shape[0] - 1).bit_length()))

    m_tiled = jnp.broadcast_to(m, (qb, LANES))
    cut_tiled = jnp.broadcast_to(cut, (qb, LANES))
    scale = AT_DH ** -0.5
    heads = [slice(h * AT_DH, (h + 1) * AT_DH) for h in range(AT_HEADS)]

    for rs, ng_u in zip(sub_rows, sub_ng):
        def logits_group(g, mrun, rs=rs, m_u=m_tiled[rs], cut_u=cut_tiled[rs]):
            bias = []
            for c in range(nc):
                s = sc_scr[g, rs, c * LANES:(c + 1) * LANES]
                sel = (s > m_u) | ((s == m_u) & (kpos_of(g, c) <= cut_u))
                bias.append(jnp.where(sel, 0.0, NEG_BIG))
            bias = jnp.concatenate(bias, axis=1)
            out = []
            for h, sl in enumerate(heads):
                t = _dot_nt(aq_ref[rs, sl], kb_ref[rows_of(g), sl]) * scale + bias
                lg_scr[h, g] = t
                mt = mrun[h]
                for c in range(nc):
                    mt = jnp.maximum(mt, t[:, c * LANES:(c + 1) * LANES])
                out.append(mt)
            return tuple(out)

        mrun = sweep(logits_group,
                     tuple(jnp.full((SUB_ROWS, LANES), NEG_BIG, F32) for _ in heads), ng_u)
        for h in range(AT_HEADS):
            w_scr[h, rs, :] = jnp.broadcast_to(jnp.max(mrun[h], axis=1, keepdims=True),
                                               (SUB_ROWS, LANES))

        def pv_group(g, carry, rs=rs):
            out = []
            for h, (sl, (lsum, acc)) in enumerate(zip(heads, carry)):
                mx = w_scr[h, rs, :]
                ps = [jnp.exp(lg_scr[h, g, :, c * LANES:(c + 1) * LANES] - mx)
                      for c in range(nc)]
                for p in ps:
                    lsum = lsum + p
                pb = jnp.concatenate([p.astype(BF16) for p in ps], axis=1)
                out.append((lsum, acc + _dot(pb, vb_ref[rows_of(g), sl])))
            return tuple(out)

        zeros = jnp.zeros((SUB_ROWS, LANES), F32)
        stats = sweep(pv_group, tuple((zeros, jnp.zeros((SUB_ROWS, AT_DH), F32)) for _ in heads),
                      ng_u)
        for sl, (lsum, acc) in zip(heads, stats):
            o_ref[rs, sl] = (acc / jnp.sum(lsum, axis=1, keepdims=True)).astype(BF16)


def _attn_prompt(aq, iq, iw, kb, vb, ikb, nb, qb):
    rows = aq.shape[0]
    t = rows // nb
    n_sel = min(TOPK_MAX, t // 4)
    nq = t // qb
    kg = min(KEY_GROUP, t)
    assert t % kg == 0 and kg % LANES == 0 and t % qb == 0 and qb % SUB_ROWS == 0
    qrow = lambda n: pl.BlockSpec((qb, n), lambda b, i: (b * nq + i, 0))
    seq = lambda n: pl.BlockSpec((t, n), lambda b, i: (b, 0), pipeline_mode=pl.Buffered(1))
    return pl.pallas_call(
        functools.partial(_attn_prompt_kernel, n_sel=n_sel),
        grid=(nb, nq),
        in_specs=[qrow(AT_W), qrow(IDX_W), qrow(IDX_HEADS), seq(AT_W), seq(AT_W), seq(IDX_DIM)],
        out_specs=qrow(AT_W),
        out_shape=jax.ShapeDtypeStruct((rows, AT_W), BF16),
        scratch_shapes=[pltpu.VMEM((t // kg, qb, kg), F32),
                        pltpu.VMEM((IDX_HEADS, qb, LANES), F32),
                        pltpu.VMEM((IDX_HEADS, qb, IDX_DIM), BF16),
                        pltpu.VMEM((AT_HEADS, t // kg, SUB_ROWS, kg), F32)],
        compiler_params=_cparams(("arbitrary", "arbitrary")),
        name="attn_prompt",
    )(aq, iq, iw, kb, vb, ikb)


def _pad_rows(x, rows):
    return jnp.concatenate([x, jnp.zeros((rows - x.shape[0], x.shape[1]), x.dtype)], axis=0)


def _select_sample_kernel(pt_ref, iq_ref, iw_ref, ikn_ref, *rest, n_pages, n_sel, t, seq_blk):
    del pt_ref
    ik_pages = rest[:n_pages]
    bias_ref, sc_scr = rest[n_pages:]
    page = ik_pages[0].shape[1]
    past = n_pages * page
    total = past + page
    nc = total // LANES
    slot = pl.program_id(0) % seq_blk
    rows = pl.ds(pl.multiple_of(slot * t, t), t)
    lane = lax.broadcasted_iota(I32, (1, LANES), 1)

    iq = iq_ref[...]
    w = iw_ref[...]

    def head_sum(d):
        r = jnp.maximum(d, 0.0) * w
        return jnp.sum(r.reshape(t, IDX_HEADS, d.shape[1]), axis=1) * IDX_SCALE

    for p in range(n_pages):
        sc_scr[rows, p * page:(p + 1) * page] = head_sum(_dot(iq, ik_pages[p][...].astype(BF16)))
    new = head_sum(_dot_nt(iq, _pad_rows(ikn_ref[...], page).astype(BF16)))
    tok_pos = lax.broadcasted_iota(I32, (t, 1), 0)
    sc_scr[rows, past:total] = jnp.where(lane <= tok_pos, new, -jnp.inf)

    @pl.when(slot == seq_blk - 1)
    def _():
        nrow = seq_blk * t
        qpos = past + lax.broadcasted_iota(I32, (nrow, 1), 0) % t
        kpos_of = lambda c: c * LANES + lane

        def reduce_keys(pred, ops, init, elem, combine, lane_reduce):
            tiled = [jnp.broadcast_to(x, (nrow, LANES)) for x in ops]
            acc = jnp.full((nrow, LANES), init, F32)
            for c in range(nc):
                s = sc_scr[:, c * LANES:(c + 1) * LANES]
                acc = combine(acc, elem(pred(s, kpos_of(c), *tiled), s))
            return lane_reduce(acc, axis=1, keepdims=True)

        count = lambda pred, *ops: reduce_keys(
            pred, ops, 0.0, lambda p, s: jnp.where(p, 1.0, 0.0), lambda a, b: a + b, jnp.sum)
        masked_min = lambda pred, *ops: reduce_keys(
            pred, ops, jnp.inf, lambda p, s: jnp.where(p, s, jnp.inf), jnp.minimum, jnp.min)
        need = jnp.minimum(qpos + 1, n_sel).astype(F32)
        m, cut = _select(need, count, masked_min, max(1, (total - 1).bit_length()))
        for c in range(nc):
            s = sc_scr[:, c * LANES:(c + 1) * LANES]
            sel = (s > m) | ((s == m) & (kpos_of(c) <= cut))
            bias_ref[:, :, c * LANES:(c + 1) * LANES] = jnp.where(sel, 0.0, NEG_BIG).reshape(
                seq_blk, t, LANES)


def _attend_sample_kernel(pt_ref, aq_ref, bias_ref, kn_ref, vn_ref, *rest, n_pages, t):
    del pt_ref
    k_pages = rest[:n_pages]
    v_pages = rest[n_pages:2 * n_pages]
    o_ref, kb_scr, vb_scr = rest[2 * n_pages:]
    page = k_pages[0].shape[0] // AT_HEADS
    past = n_pages * page
    total = past + page

    for p in range(n_pages):
        rows = slice(p * page, (p + 1) * page)
        for h in range(AT_HEADS):
            head_rows = pl.ds(h, page, stride=AT_HEADS)
            kb_scr[h, rows, :] = k_pages[p][head_rows, :].astype(BF16)
            vb_scr[h, rows, :] = v_pages[p][head_rows, :].astype(BF16)
    tail = slice(past, total)
    for h in range(AT_HEADS):
        sl = slice(h * AT_DH, (h + 1) * AT_DH)
        kb_scr[h, tail, :] = _pad_rows(kn_ref[:, sl], page).astype(BF16)
        vb_scr[h, tail, :] = _pad_rows(vn_ref[:, sl], page).astype(BF16)

    scale = AT_DH ** -0.5
    aq = aq_ref[...].astype(BF16)
    bias = bias_ref[...]
    for h in range(AT_HEADS):
        sl = slice(h * AT_DH, (h + 1) * AT_DH)
        s = _dot_nt(aq[:, sl], kb_scr[h]) * scale + bias
        p = jnp.exp(s - jnp.max(s, axis=1, keepdims=True))
        l = jnp.sum(p, axis=1, keepdims=True)
        o_ref[:, sl] = _dot(p.astype(BF16), vb_scr[h]) / l


def _attn_sample(page_table, aq, iq, iw, kn, vn, ikn, cache_ikt, cache_k, cache_v, t_real):
    nseq, n_pages = page_table.shape
    page = cache_ikt.shape[2]
    past = n_pages * page
    total = past + page
    t = aq.shape[1]
    n_sel = min(TOPK_MAX, (past + t_real) // 4)
    seq_blk = min(SELECT_SEQS, nseq)
    assert nseq % seq_blk == 0
    tok = lambda r, n: pl.BlockSpec((None, r, n), lambda s, pt: (s, 0, 0))
    pg = lambda r, n, p: pl.BlockSpec((None, r, n), lambda s, pt, p=p: (pt[s, p], 0, 0))

    bias = pl.pallas_call(
        functools.partial(_select_sample_kernel, n_pages=n_pages, n_sel=n_sel, t=t,
                          seq_blk=seq_blk),
        grid_spec=pltpu.PrefetchScalarGridSpec(
            num_scalar_prefetch=1,
            grid=(nseq,),
            in_specs=[tok(t * IDX_HEADS, IDX_DIM), tok(t * IDX_HEADS, 1), tok(t, IDX_DIM)]
            + [pg(IDX_DIM, page, p) for p in range(n_pages)],
            out_specs=pl.BlockSpec((seq_blk, t, total), lambda s, pt: (s // seq_blk, 0, 0)),
            scratch_shapes=[pltpu.VMEM((seq_blk * t, total), F32)]),
        out_shape=jax.ShapeDtypeStruct((nseq, t, total), F32),
        compiler_params=_cparams(("arbitrary",)),
        name="select_sample",
    )(page_table, iq, iw, ikn, *([cache_ikt] * n_pages))

    return pl.pallas_call(
        functools.partial(_attend_sample_kernel, n_pages=n_pages, t=t),
        grid_spec=pltpu.PrefetchScalarGridSpec(
            num_scalar_prefetch=1,
            grid=(nseq,),
            in_specs=[tok(t, AT_W), tok(t, total), tok(t, AT_W), tok(t, AT_W)]
            + [pg(page * AT_HEADS, AT_DH, p) for p in range(n_pages)] * 2,
            out_specs=tok(t, AT_W),
            scratch_shapes=[pltpu.VMEM((AT_HEADS, total, AT_DH), BF16),
                            pltpu.VMEM((AT_HEADS, total, AT_DH), BF16)]),
        out_shape=jax.ShapeDtypeStruct((nseq, t, AT_W), F32),
        compiler_params=_cparams(("arbitrary",)),
        name="attend_sample",
    )(page_table, aq, bias, kn, vn, *([cache_k] * n_pages), *([cache_v] * n_pages))


def _out_kernel(x_ref, og_ref, ao_ref, gg_ref, wa_ref, wb_ref, wo_ref, n2_ref, wu_ref, wd_ref,
                nf_ref, y_ref):
    d = x_ref.shape[1]
    ya = _dot(og_ref[...], wa_ref[...])
    yb = _dot(ao_ref[...], wb_ref[...])
    m = _sigmoid(gg_ref[:, :d]) * ya + _sigmoid(gg_ref[:, d:]) * yb
    x1 = x_ref[...] + _dot(m.astype(BF16), wo_ref[...])
    h2 = _rms(x1, n2_ref[...]).astype(BF16)
    acc = x1
    ff = wu_ref.shape[1]
    for c in range(ff // d):
        cs = slice(c * d, (c + 1) * d)
        u = jnp.maximum(_dot(h2, wu_ref[:, cs]), 0.0)
        acc = acc + _dot((u * u).astype(BF16), wd_ref[cs, :])
    y_ref[...] = _rms(acc, nf_ref[...])


def _out_mlp(x, og, ao, gg, wa, wb, wo, n2, wu, wd, nf, tm):
    rows, d = x.shape
    row = lambda n: pl.BlockSpec((tm, n), lambda i: (i, 0))
    const = lambda a: pl.BlockSpec(a.shape, lambda i: (0, 0), pipeline_mode=pl.Buffered(1))
    return pl.pallas_call(
        _out_kernel,
        grid=(rows // tm,),
        in_specs=[row(d), row(HG_W), row(AT_W), row(2 * d), const(wa), const(wb), const(wo),
                  const(n2), const(wu), const(wd), const(nf)],
        out_specs=row(d),
        out_shape=jax.ShapeDtypeStruct((rows, d), F32),
        compiler_params=_cparams(("arbitrary",)),
        name="out_mlp",
    )(x, og, ao, gg, wa, wb, wo, n2, wu, wd, nf)


def _pack_w_in(w):
    d = w.shape[0]
    pad = lambda n: jnp.zeros((d, n), w.dtype)
    ik0 = _C_IK
    iw0 = ik0 + IDX_DIM
    gg0 = iw0 + IDX_HEADS
    return jnp.concatenate(
        [w[:, :ik0], w[:, ik0:iw0], pad(LANES - IDX_DIM), w[:, iw0:gg0], pad(LANES - IDX_HEADS),
         w[:, gg0:]], axis=1).astype(BF16)


def kernel(x_prompt, x_sample, cache_k, cache_v, cache_idx_k, state_hgrn, page_table, lb_logits,
           w_in, hg_norm, w_a, w_b, w_o, norm1, norm2, w_up, w_down, norm_f):
    depth = w_in.shape[0]
    assert depth == 1, "single-layer stack"
    nb, t, d = x_prompt.shape
    ns, ts, _ = x_sample.shape
    n_phys, page = cache_k.shape[1], cache_k.shape[2]

    w_in_p = _pack_w_in(w_in[0])
    assert w_in_p.shape[1] == _C_END
    bf = lambda a: a.astype(BF16)
    wa, wb, wo, wu, wd = bf(w_a[0]), bf(w_b[0]), bf(w_o[0]), bf(w_up[0]), bf(w_down[0])
    n1, n2, nf = norm1[0][None], norm2[0][None], norm_f[None]
    hgn = hg_norm[0][None]

    xp = x_prompt.reshape(nb * t, d)
    xs = x_sample.reshape(ns * ts, d)
    tm_p, tm_s = min(ROW_TILE, nb * t), min(ROW_TILE, ns * ts)
    zh_p, aq_p, ak_p, av_p, akb_p, avb_p, iq_p, ik_p, ikb_p, iw_p, gg_p = _inproj(xp, n1, w_in_p, tm_p)
    zh_s, aq_s, ak_s, av_s, _, _, iq_s, ik_s, _, iw_s, gg_s = _inproj(xs, n1, w_in_p, tm_s)

    og_p, st_p = _hgrn_prompt(lb_logits, hgn, zh_p.reshape(nb, t, 4 * HG_W))
    og_s, st_s = _hgrn_sample(lb_logits, hgn, zh_s, state_hgrn[0], ts, 8)

    ao_p = _attn_prompt(aq_p, iq_p, iw_p, akb_p, avb_p, ikb_p, nb, min(SELECT_ROWS, t))

    tp = -(-ts // 8) * 8
    tok = lambda a: jnp.pad(a.reshape(ns, ts, -1), ((0, 0), (0, tp - ts), (0, 0)))
    ao_s = _attn_sample(
        page_table,
        tok(aq_s.astype(F32)),
        tok(iq_s).reshape(ns, tp * IDX_HEADS, IDX_DIM),
        tok(iw_s).reshape(ns, tp * IDX_HEADS, 1),
        tok(ak_s), tok(av_s), tok(ik_s),
        jnp.swapaxes(cache_idx_k, 2, 3).reshape(n_phys, IDX_DIM, page),
        cache_k.reshape(n_phys, page * AT_HEADS, AT_DH), cache_v.reshape(n_phys, page * AT_HEADS, AT_DH),
        ts)
    ao_s = ao_s[:, :ts].reshape(ns * ts, AT_W).astype(BF16)

    y_p = _out_mlp(xp, og_p.reshape(nb * t, HG_W), ao_p, gg_p, wa, wb, wo, n2, wu, wd, nf, tm_p)
    y_s = _out_mlp(xs, og_s, ao_s, gg_s, wa, wb, wo, n2, wu, wd, nf, tm_s)

    return (y_p.reshape(nb, t, d), y_s.reshape(ns, ts, d),
            ak_p.reshape(1, nb, t, AT_HEADS, AT_DH), av_p.reshape(1, nb, t, AT_HEADS, AT_DH),
            ik_p.reshape(1, nb, t, IDX_DIM), st_p[None],
            ak_s.reshape(1, ns, ts, AT_HEADS, AT_DH), av_s.reshape(1, ns, ts, AT_HEADS, AT_DH),
            ik_s.reshape(1, ns, ts, IDX_DIM), st_s[None])
```
